```python
import math
import jax, jax.numpy as jnp
from jax import lax
import numpy as np

D_MODEL = 1024
BATCH = 16
SEQ = 2048
DEPTH = 1
DEC_BATCH = 16
DEC_SEQ = 4096
PAST_LEN = 128

GRID_W = 64
N_Q_HEADS = 16
N_KV_HEADS = 4
Q_PER_KV = N_Q_HEADS // N_KV_HEADS
HEAD_DIM = 64
ATT_W = N_Q_HEADS * HEAD_DIM
KV_W = N_KV_HEADS * HEAD_DIM
AXIS_DIM = HEAD_DIM // 2
ROPE_THETA = 10000.0
Q_BLOCK = 128
SSM_EXPAND = 2
D_INNER = SSM_EXPAND * D_MODEL
SSM_HEAD_DIM = 64
N_SSM_HEADS = D_INNER // SSM_HEAD_DIM
N_SSM_GROUPS = 4
D_STATE = 128
CONV_W = 5
CHUNK = 128
CONV_CH = D_INNER + 2 * N_SSM_GROUPS * D_STATE
DT_MIN = 0.001
DT_MAX = 0.1
A_MIN = 1.0
A_MAX = 16.0
RMS_EPS = 1e-6
LN_EPS = 1e-5
ALPHA = (2 * DEPTH) ** 0.25
INIT_BETA = (8 * DEPTH) ** -0.25
SPLITS = (ATT_W, KV_W, KV_W, ATT_W, D_INNER, CONV_CH, 2 * N_SSM_HEADS, 2 * D_MODEL)
IN_W = sum(SPLITS)
SPLIT_POINTS = tuple(int(v) for v in np.cumsum(SPLITS)[:-1])

kernel_name = 'hybrid_gqa_ssd_encoder'


def _rms_norm(x, w):
    xf = x.astype(jnp.float32)
    xf = xf * lax.rsqrt(jnp.mean(xf * xf, axis=-1, keepdims=True) + RMS_EPS)
    return xf.astype(x.dtype) * w


def _layer_norm(x, g, b):
    xf = x.astype(jnp.float32)
    mu = jnp.mean(xf, axis=-1, keepdims=True)
    var = jnp.mean(jnp.square(xf - mu), axis=-1, keepdims=True)
    return ((xf - mu) * lax.rsqrt(var + LN_EPS)).astype(x.dtype) * g + b


def _axial_rope(seq_len):
    rows = seq_len // GRID_W
    row = jnp.repeat(jnp.arange(rows, dtype=jnp.float32), GRID_W)
    col = jnp.tile(jnp.arange(GRID_W, dtype=jnp.float32), rows)
    freqs = ROPE_THETA ** (-jnp.arange(0, AXIS_DIM, 2, dtype=jnp.float32) / AXIS_DIM)
    ang = jnp.concatenate([row[:, None] * freqs, col[:, None] * freqs], axis=-1)
    return jnp.cos(ang), jnp.sin(ang)


def _apply_rope(x, cos, sin):
    xf = x.astype(jnp.float32).reshape(*x.shape[:-1], HEAD_DIM // 2, 2)
    x0, x1 = xf[..., 0], xf[..., 1]
    c, s = cos[:, None, :], sin[:, None, :]
    out = jnp.stack([x0 * c - x1 * s, x0 * s + x1 * c], axis=-1)
    return out.reshape(x.shape).astype(x.dtype)


def _block_attention(q, k, v):
    b, S = q.shape[0], q.shape[1]
    nb = S // Q_BLOCK
    qb = q.reshape(b, nb, Q_BLOCK, N_KV_HEADS, Q_PER_KV, HEAD_DIM).transpose(1, 0, 2, 3, 4, 5)

    def one_block(qi):
        s = jnp.einsum('bqgrd,bkgd->bgrqk', qi, k).astype(jnp.float32)
        p = jax.nn.softmax(s, axis=-1).astype(v.dtype)
        return jnp.einsum('bgrqk,bkgd->bqgrd', p, v)

    o = lax.map(one_block, qb)
    return o.transpose(1, 0, 2, 3, 4, 5).reshape(b, S, ATT_W)


def _depthwise_conv(x, w, bias):
    c = x.shape[-1]
    y = lax.conv_general_dilated(x, w[:, None, :].astype(x.dtype), window_strides=(1,),
                                 padding=[(CONV_W // 2, CONV_W // 2)],
                                 dimension_numbers=('NWC', 'WIO', 'NWC'),
                                 feature_group_count=c)
    return y + bias


def _ssd_chunked(x, dt, a, bm, cm):
    b, L, H, P = x.shape
    G, N = bm.shape[2], bm.shape[3]
    R = H // G
    nc = L // CHUNK
    xg = x.astype(jnp.float32).reshape(b, nc, CHUNK, G, R, P)
    bc = bm.astype(jnp.float32).reshape(b, nc, CHUNK, G, N)
    cc = cm.astype(jnp.float32).reshape(b, nc, CHUNK, G, N)
    dtc = dt.reshape(b, nc, CHUNK, G, R)
    acs = jnp.cumsum(dtc * a.reshape(G, R), axis=2)
    lower = jnp.tril(jnp.ones((CHUNK, CHUNK), dtype=bool))
    seg = acs[:, :, :, None] - acs[:, :, None]
    lmat = jnp.exp(jnp.where(lower[:, :, None, None], seg, -jnp.inf))
    cb = jnp.einsum('bcign,bcjgn->bcijg', cc, bc)
    scores = cb[..., None] * lmat * dtc[:, :, None]
    y_diag = jnp.einsum('bcijgr,bcjgrp->bcigrp', scores, xg)
    xw = xg * (jnp.exp(acs[:, :, -1:] - acs) * dtc)[..., None]
    states = jnp.einsum('bcjgn,bcjgrp->bcgrpn', bc, xw)
    chunk_decay = jnp.exp(acs[:, :, -1])

    def step(h, inp):
        st, dec = inp
        return dec[..., None, None] * h + st, h

    h0 = jnp.zeros((b, G, R, P, N), jnp.float32)
    _, h_prev = lax.scan(step, h0, (jnp.moveaxis(states, 1, 0), jnp.moveaxis(chunk_decay, 1, 0)))
    h_prev = jnp.moveaxis(h_prev, 0, 1)
    y_off = jnp.einsum('bcign,bcgrpn->bcigrp', cc, h_prev) * jnp.exp(acs)[..., None]
    return (y_diag + y_off).reshape(b, L, H, P)


def _hybrid_layer(x, w_in, b_gate, q_norm_w, k_norm_w, conv_w, conv_b, dt_bias_fwd, dt_bias_bwd,
                  a_log_fwd, a_log_bwd, d_skip, ssm_norm_w, w_att_proj, w_ssm_proj, w_out, ln_g, ln_b):
    b, S, _ = x.shape
    proj = jnp.einsum('bsd,de->bse', x, w_in)
    q, k, v, g_att, z, xbc, dt_raw, gate_raw = jnp.split(proj, SPLIT_POINTS, axis=-1)

    q = _rms_norm(q.reshape(b, S, N_Q_HEADS, HEAD_DIM), q_norm_w)
    k = _rms_norm(k.reshape(b, S, N_KV_HEADS, HEAD_DIM), k_norm_w)
    v = v.reshape(b, S, N_KV_HEADS, HEAD_DIM)
    cos, sin = _axial_rope(S)
    q = _apply_rope(q, cos, sin) * (HEAD_DIM ** -0.5)
    k = _apply_rope(k, cos, sin)
    att = _block_attention(q, k, v) * jax.nn.silu(g_att)
    att = jnp.einsum('bse,ed->bsd', att, w_att_proj)

    xbc = jax.nn.silu(_depthwise_conv(xbc, conv_w, conv_b))
    xs, bm, cm = jnp.split(xbc, [D_INNER, D_INNER + N_SSM_GROUPS * D_STATE], axis=-1)
    xs = xs.reshape(b, S, N_SSM_HEADS, SSM_HEAD_DIM)
    bm = bm.reshape(b, S, N_SSM_GROUPS, D_STATE)
    cm = cm.reshape(b, S, N_SSM_GROUPS, D_STATE)
    dt_f, dt_b = jnp.split(dt_raw.astype(jnp.float32), 2, axis=-1)
    dt_f = jax.nn.softplus(dt_f + dt_bias_fwd.astype(jnp.float32))
    dt_b = jax.nn.softplus(dt_b + dt_bias_bwd.astype(jnp.float32))
    a_f = -jnp.exp(a_log_fwd.astype(jnp.float32))
    a_b = -jnp.exp(a_log_bwd.astype(jnp.float32))
    y_f = _ssd_chunked(xs, dt_f, a_f, bm, cm)
    flip = lambda t: jnp.flip(t, axis=1)
    y_b = flip(_ssd_chunked(flip(xs), flip(dt_b), a_b, flip(bm), flip(cm)))
    y = y_f + y_b + xs.astype(jnp.float32) * d_skip.astype(jnp.float32)[:, None]
    y = y.reshape(b, S, D_INNER).astype(x.dtype) * jax.nn.silu(z)
    y = _rms_norm(y.reshape(b, S, N_SSM_GROUPS, D_INNER // N_SSM_GROUPS),
                  ssm_norm_w.reshape(N_SSM_GROUPS, D_INNER // N_SSM_GROUPS)).reshape(b, S, D_INNER)
    ssm = jnp.einsum('bse,ed->bsd', y, w_ssm_proj)

    gates = jax.nn.sigmoid(gate_raw + b_gate)
    g_a, g_s = jnp.split(gates, 2, axis=-1)
    mixed = g_a * att + g_s * ssm
    out = jnp.einsum('bsd,de->bse', mixed, w_out)
    return _layer_norm(ALPHA * x + out, ln_g, ln_b)


def setup_inputs(seed: int = 0) -> dict:
    key = jax.random.key(seed)
    keys = jax.random.split(key, 20)
    f32 = jnp.float32

    def nrm(k, shape, scale):
        return jax.random.normal(k, shape, f32) * scale

    def dt_bias(k):
        dt = jnp.exp(jax.random.uniform(k, (DEPTH, N_SSM_HEADS), f32, math.log(DT_MIN), math.log(DT_MAX)))
        return dt + jnp.log(-jnp.expm1(-dt))

    return {
        'x_prompt': nrm(keys[0], (BATCH, SEQ, D_MODEL), 1.0),
        'x_sample': nrm(keys[1], (DEC_BATCH, DEC_SEQ, D_MODEL), 1.0),
        'w_in': nrm(keys[2], (DEPTH, D_MODEL, IN_W), D_MODEL ** -0.5),
        'b_gate': nrm(keys[3], (DEPTH, 2 * D_MODEL), 0.02),
        'q_norm_w': 1.0 + nrm(keys[4], (DEPTH, HEAD_DIM), 0.02),
        'k_norm_w': 1.0 + nrm(keys[5], (DEPTH, HEAD_DIM), 0.02),
        'conv_w': nrm(keys[6], (DEPTH, CONV_W, CONV_CH), CONV_W ** -0.5),
        'conv_b': nrm(keys[7], (DEPTH, CONV_CH), 0.02),
        'dt_bias_fwd': dt_bias(keys[8]),
        'dt_bias_bwd': dt_bias(keys[9]),
        'a_log_fwd': jnp.log(jax.random.uniform(keys[10], (DEPTH, N_SSM_HEADS), f32, A_MIN, A_MAX)),
        'a_log_bwd': jnp.log(jax.random.uniform(keys[11], (DEPTH, N_SSM_HEADS), f32, A_MIN, A_MAX)),
        'd_skip': 1.0 + nrm(keys[12], (DEPTH, N_SSM_HEADS), 0.1),
        'ssm_norm_w': 1.0 + nrm(keys[13], (DEPTH, D_INNER), 0.02),
        'w_att_proj': nrm(keys[14], (DEPTH, ATT_W, D_MODEL), ATT_W ** -0.5 * INIT_BETA),
        'w_ssm_proj': nrm(keys[15], (DEPTH, D_INNER, D_MODEL), D_INNER ** -0.5 * INIT_BETA),
        'w_out': nrm(keys[16], (DEPTH, D_MODEL, D_MODEL), D_MODEL ** -0.5 * INIT_BETA),
        'ln_g': 1.0 + nrm(keys[17], (DEPTH, D_MODEL), 0.02),
        'ln_b': nrm(keys[18], (DEPTH, D_MODEL), 0.02),
    }


def reference(x_prompt, x_sample, w_in, b_gate, q_norm_w, k_norm_w, conv_w, conv_b, dt_bias_fwd,
              dt_bias_bwd, a_log_fwd, a_log_bwd, d_skip, ssm_norm_w, w_att_proj, w_ssm_proj, w_out,
              ln_g, ln_b):
    xp, xs = x_prompt, x_sample
    for l in range(DEPTH):
        params = (w_in[l], b_gate[l], q_norm_w[l], k_norm_w[l], conv_w[l], conv_b[l],
                  dt_bias_fwd[l], dt_bias_bwd[l], a_log_fwd[l], a_log_bwd[l], d_skip[l],
                  ssm_norm_w[l], w_att_proj[l], w_ssm_proj[l], w_out[l], ln_g[l], ln_b[l])
        xp = _hybrid_layer(xp, *params)
        xs = _hybrid_layer(xs, *params)
    y_prompt, y_sample = xp, xs
    return (y_prompt, y_sample)
```

```python
import functools

import numpy as np
import jax
import jax.numpy as jnp
from jax import lax
from jax.experimental import pallas as pl
from jax.experimental.pallas import tpu as pltpu

F32 = jnp.float32
BF16 = jnp.bfloat16

D_MODEL = 1024
GRID_W = 64
N_Q_HEADS = 16
N_KV_HEADS = 4
HEAD_DIM = 64
ATT_W = N_Q_HEADS * HEAD_DIM
KV_W = N_KV_HEADS * HEAD_DIM
AXIS_DIM = HEAD_DIM // 2
ROPE_THETA = 10000.0
D_INNER = 2 * D_MODEL
SSM_HEAD_DIM = 64
N_SSM_HEADS = D_INNER // SSM_HEAD_DIM
N_SSM_GROUPS = 4
D_STATE = 128
CONV_W = 5
CHUNK = 128
BC_W = N_SSM_GROUPS * D_STATE
CONV_CH = D_INNER + 2 * BC_W
RMS_EPS = 1e-6
LN_EPS = 1e-5
DEPTH = 1
ALPHA = (2 * DEPTH) ** 0.25
SPLITS = (ATT_W, KV_W, KV_W, ATT_W, D_INNER, CONV_CH, 2 * N_SSM_HEADS, 2 * D_MODEL)
SPLIT_POINTS = tuple(int(v) for v in np.cumsum(SPLITS)[:-1])

LANES = 128
HALO = 8
BF16_ROWS = 16
GROUP_W = D_INNER // N_SSM_GROUPS
HEADS_PER_GROUP = N_SSM_HEADS // N_SSM_GROUPS
VMEM_LIMIT = 48 * 1024 * 1024


def _cparams(sem):
    return pltpu.CompilerParams(dimension_semantics=sem, vmem_limit_bytes=VMEM_LIMIT)


def _dot(a, b):
    return jnp.dot(a, b, preferred_element_type=F32)


def _dot_nt(a, b):
    return lax.dot_general(a, b, (((1,), (1,)), ((), ())), preferred_element_type=F32)


def _dot_tn(a, b):
    return lax.dot_general(a, b, (((0,), (0,)), ((), ())), preferred_element_type=F32)


def _silu(x):
    return x / (1.0 + jnp.exp(-x))


def _split2(v):
    hi = v.astype(BF16)
    lo = (v - hi.astype(F32)).astype(BF16)
    return hi, lo


def _split3(v):
    hi = v.astype(BF16)
    r = v - hi.astype(F32)
    mid = r.astype(BF16)
    lo = (r - mid.astype(F32)).astype(BF16)
    return hi, mid, lo


def _qkv_kernel(x_ref, w_ref, cos_ref, sin_ref, qw_ref, kw_ref, bd_ref, q_ref, kt_ref, v_ref):
    tm = x_ref.shape[1]
    xb = x_ref[0].astype(BF16)
    cosv = cos_ref[...]
    sinv = sin_ref[...]
    bd = bd_ref[...]
    lane = lax.broadcasted_iota(jnp.int32, (tm, LANES), 1)
    second_half = (lane & (HEAD_DIM // 2)) != 0

    def norm_rope(acc, w, scale):
        hi, lo = _split2(acc * acc)
        ms = _dot(hi, bd) + _dot(lo, bd)
        xn = acc * lax.rsqrt(ms + RMS_EPS) * w
        outs = []
        for j in range(xn.shape[1] // LANES):
            xj = xn[:, j * LANES:(j + 1) * LANES]
            partner = jnp.where(second_half,
                                pltpu.roll(xj, HEAD_DIM // 2, 1),
                                pltpu.roll(xj, LANES - HEAD_DIM // 2, 1))
            outs.append((xj * cosv + partner * sinv) * scale)
        return jnp.concatenate(outs, axis=1)

    cw = 2 * LANES
    for c in range(ATT_W // cw):
        acc = _dot(xb, w_ref[:, c * cw:(c + 1) * cw])
        q_ref[0, :, c * cw:(c + 1) * cw] = norm_rope(acc, qw_ref[...], HEAD_DIM ** -0.5).astype(BF16)
    acc = _dot(xb, w_ref[:, ATT_W:ATT_W + KV_W])
    kt_ref[0] = norm_rope(acc, kw_ref[...], 1.0).T.astype(BF16)
    acc = _dot(xb, w_ref[:, ATT_W + KV_W:ATT_W + 2 * KV_W])
    for g in range(N_KV_HEADS):
        v_ref[0, g] = acc[:, g * HEAD_DIM:(g + 1) * HEAD_DIM].astype(BF16)


def _qkv_call(x, w_qkv, cos_t, sin_t, qw, kw, bd, tm):
    b, s, _ = x.shape
    return pl.pallas_call(
        _qkv_kernel,
        grid=(b, s // tm),
        in_specs=[
            pl.BlockSpec((1, tm, D_MODEL), lambda i, j: (i, j, 0)),
            pl.BlockSpec(w_qkv.shape, lambda i, j: (0, 0)),
            pl.BlockSpec((tm, LANES), lambda i, j: (j, 0)),
            pl.BlockSpec((tm, LANES), lambda i, j: (j, 0)),
            pl.BlockSpec(qw.shape, lambda i, j: (0, 0)),
            pl.BlockSpec(kw.shape, lambda i, j: (0, 0)),
            pl.BlockSpec(bd.shape, lambda i, j: (0, 0)),
        ],
        out_specs=[
            pl.BlockSpec((1, tm, ATT_W), lambda i, j: (i, j, 0)),
            pl.BlockSpec((1, KV_W, tm), lambda i, j: (i, 0, j)),
            pl.BlockSpec((1, N_KV_HEADS, tm, HEAD_DIM), lambda i, j: (i, 0, j, 0)),
        ],
        out_shape=[
            jax.ShapeDtypeStruct((b, s, ATT_W), BF16),
            jax.ShapeDtypeStruct((b, KV_W, s), BF16),
            jax.ShapeDtypeStruct((b, N_KV_HEADS, s, HEAD_DIM), BF16),
        ],
        compiler_params=_cparams(("parallel", "parallel")),
        name="qkv_proj",
    )(x, w_qkv, cos_t, sin_t, qw, kw, bd)


def _gz_kernel(x_ref, w_ref, g_ref, z_ref):
    xb = x_ref[0].astype(BF16)
    cw = 4 * LANES
    for c in range(ATT_W // cw):
        acc = _dot(xb, w_ref[:, c * cw:(c + 1) * cw])
        g_ref[0, :, c * cw:(c + 1) * cw] = _silu(acc).astype(BF16)
    for c in range(D_INNER // cw):
        acc = _dot(xb, w_ref[:, ATT_W + c * cw:ATT_W + (c + 1) * cw])
        z_ref[0, :, c * cw:(c + 1) * cw] = _silu(acc).astype(BF16)


def _gz_call(x, w_gz, tm):
    b, s, _ = x.shape
    return pl.pallas_call(
        _gz_kernel,
        grid=(b, s // tm),
        in_specs=[
            pl.BlockSpec((1, tm, D_MODEL), lambda i, j: (i, j, 0)),
            pl.BlockSpec(w_gz.shape, lambda i, j: (0, 0)),
        ],
        out_specs=[
            pl.BlockSpec((1, tm, ATT_W), lambda i, j: (i, j, 0)),
            pl.BlockSpec((1, tm, D_INNER), lambda i, j: (i, j, 0)),
        ],
        out_shape=[
            jax.ShapeDtypeStruct((b, s, ATT_W), BF16),
            jax.ShapeDtypeStruct((b, s, D_INNER), BF16),
        ],
        compiler_params=_cparams(("parallel", "parallel")),
        name="gate_proj",
    )(x, w_gz)


def _xbc_kernel(x_ref, xp_ref, xn_ref, w_ref, cw_ref, cb_ref, xs_ref, b_ref, c_ref, acc_ref):
    tm = x_ref.shape[1]
    j = pl.program_id(1)
    nj = pl.num_programs(1)
    prev = jnp.where(j > 0, xp_ref[0], 0.0)
    nxt = jnp.where(j < nj - 1, xn_ref[0], 0.0)
    xall = jnp.concatenate([prev, x_ref[0], nxt], axis=0).astype(BF16)
    cw = 4 * LANES
    outs = ([(xs_ref, c * cw) for c in range(D_INNER // cw)] + [(b_ref, 0), (c_ref, 0)])
    for c, (o_ref, off) in enumerate(outs):
        acc_ref[...] = _dot(xall, w_ref[:, c * cw:(c + 1) * cw])
        y = cb_ref[:, c * cw:(c + 1) * cw]
        for k in range(CONV_W):
            r0 = HALO - CONV_W // 2 + k
            y = y + acc_ref[r0:r0 + tm, :] * cw_ref[k:k + 1, c * cw:(c + 1) * cw]
        o_ref[0, :, off:off + cw] = _silu(y).astype(BF16)


def _xbc_call(x, w_xbc, conv_w, conv_b, tm):
    b, s, _ = x.shape
    hb = tm // HALO
    last = s // HALO - 1
    cw = 4 * LANES
    return pl.pallas_call(
        _xbc_kernel,
        grid=(b, s // tm),
        in_specs=[
            pl.BlockSpec((1, tm, D_MODEL), lambda i, j: (i, j, 0)),
            pl.BlockSpec((1, HALO, D_MODEL), lambda i, j: (i, jnp.maximum(j * hb - 1, 0), 0)),
            pl.BlockSpec((1, HALO, D_MODEL), lambda i, j: (i, jnp.minimum((j + 1) * hb, last), 0)),
            pl.BlockSpec(w_xbc.shape, lambda i, j: (0, 0)),
            pl.BlockSpec(conv_w.shape, lambda i, j: (0, 0)),
            pl.BlockSpec(conv_b.shape, lambda i, j: (0, 0)),
        ],
        out_specs=[
            pl.BlockSpec((1, tm, D_INNER), lambda i, j: (i, j, 0)),
            pl.BlockSpec((1, tm, BC_W), lambda i, j: (i, j, 0)),
            pl.BlockSpec((1, tm, BC_W), lambda i, j: (i, j, 0)),
        ],
        out_shape=[
            jax.ShapeDtypeStruct((b, s, D_INNER), BF16),
            jax.ShapeDtypeStruct((b, s, BC_W), BF16),
            jax.ShapeDtypeStruct((b, s, BC_W), BF16),
        ],
        scratch_shapes=[pltpu.VMEM((tm + 2 * HALO, cw), F32)],
        compiler_params=_cparams(("parallel", "parallel")),
        name="xbc_conv_proj",
    )(x, x, x, w_xbc, conv_w, conv_b)


def _gdt_kernel(x_ref, w_ref, bg_ref, bdt_ref, gate_ref, dt_ref):
    xb = x_ref[0].astype(BF16)
    cw = 4 * LANES
    for c in range(2 * D_MODEL // cw):
        acc = _dot(xb, w_ref[:, c * cw:(c + 1) * cw]) + bg_ref[:, c * cw:(c + 1) * cw]
        gate_ref[0, :, c * cw:(c + 1) * cw] = (1.0 / (1.0 + jnp.exp(-acc))).astype(BF16)
    v = _dot(xb, w_ref[:, 2 * D_MODEL:2 * D_MODEL + LANES]) + bdt_ref[...]
    dt_ref[0] = jnp.maximum(v, 0.0) + jnp.log1p(jnp.exp(-jnp.abs(v)))


def _gdt_call(x, w_gdt, b_gate, b_dt, tm):
    b, s, _ = x.shape
    return pl.pallas_call(
        _gdt_kernel,
        grid=(b, s // tm),
        in_specs=[
            pl.BlockSpec((1, tm, D_MODEL), lambda i, j: (i, j, 0)),
            pl.BlockSpec(w_gdt.shape, lambda i, j: (0, 0)),
            pl.BlockSpec(b_gate.shape, lambda i, j: (0, 0)),
            pl.BlockSpec(b_dt.shape, lambda i, j: (0, 0)),
        ],
        out_specs=[
            pl.BlockSpec((1, tm, 2 * D_MODEL), lambda i, j: (i, j, 0)),
            pl.BlockSpec((1, tm, LANES), lambda i, j: (i, j, 0)),
        ],
        out_shape=[
            jax.ShapeDtypeStruct((b, s, 2 * D_MODEL), BF16),
            jax.ShapeDtypeStruct((b, s, LANES), F32),
        ],
        compiler_params=_cparams(("parallel", "parallel")),
        name="gate_dt_proj",
    )(x, w_gdt, b_gate, b_dt)


def _attn_kernel(q_ref, kt_ref, v_ref, g_ref, o_ref):
    kt = kt_ref[0]
    v = v_ref[0, 0]
    outs = []
    for r in range(N_Q_HEADS // N_KV_HEADS):
        q = q_ref[0, :, r * HEAD_DIM:(r + 1) * HEAD_DIM]
        s = _dot(q, kt)
        m = jnp.max(s, axis=-1, keepdims=True)
        p = jnp.exp(s - m)
        l = jnp.sum(p, axis=-1, keepdims=True)
        outs.append(_dot(p.astype(BF16), v) / l)
    o = jnp.concatenate(outs, axis=1)
    o_ref[0] = (o * g_ref[0].astype(F32)).astype(BF16)


def _attn_call(q, kt, v, gsil, tq):
    b, s, _ = q.shape
    gw = ATT_W // N_KV_HEADS
    return pl.pallas_call(
        _attn_kernel,
        grid=(b, N_KV_HEADS, s // tq),
        in_specs=[
            pl.BlockSpec((1, tq, gw), lambda i, g, j: (i, j, g)),
            pl.BlockSpec((1, HEAD_DIM, s), lambda i, g, j: (i, g, 0)),
            pl.BlockSpec((1, 1, s, HEAD_DIM), lambda i, g, j: (i, g, 0, 0)),
            pl.BlockSpec((1, tq, gw), lambda i, g, j: (i, j, g)),
        ],
        out_specs=pl.BlockSpec((1, tq, gw), lambda i, g, j: (i, j, g)),
        out_shape=jax.ShapeDtypeStruct((b, s, ATT_W), BF16),
        compiler_params=_cparams(("parallel", "parallel", "parallel")),
        name="attention",
    )(q, kt, v, gsil)


def _pack_hilo(v, lane):
    hi = v.astype(BF16).astype(F32)
    lo = (v - hi).astype(BF16).astype(F32)
    return jnp.where(lane < LANES // 2, hi, pltpu.roll(lo, LANES // 2, 1)).astype(BF16)


def _chunk_decay_terms(dt, alog_ref, tril_ref):
    a = -jnp.exp(alog_ref[...])
    da = dt * a
    tril = tril_ref[...]
    hi, mid, lo = _split3(da)
    pre = _dot(tril, hi) + _dot(tril, mid) + _dot(tril, lo)
    tot = pre[CHUNK - 1:CHUNK, :]
    return da, pre, tot


def _ssd_bwd_kernel(x_ref, b_ref, dt_ref, alog_ref, tril_ref, eb_ref, hb_ref, h_scr):
    c = pl.program_id(1)

    @pl.when(c == 0)
    def _():
        h_scr[...] = jnp.zeros_like(h_scr)

    hb_ref[0, 0] = h_scr[...].astype(BF16)
    dt = dt_ref[0]
    lane = lax.broadcasted_iota(jnp.int32, (CHUNK, LANES), 1)
    da, pre, tot = _chunk_decay_terms(dt, alog_ref, tril_ref)
    wgt = jnp.exp(pre - da) * dt
    dec = jnp.exp(jnp.broadcast_to(tot, (BF16_ROWS, LANES)))
    packed = _pack_hilo(jnp.concatenate([wgt, dec], axis=0),
                        lax.broadcasted_iota(jnp.int32, (CHUNK + BF16_ROWS, LANES), 1))
    ex = _dot(packed, eb_ref[...])
    xw = (x_ref[0].astype(F32) * ex[:CHUNK]).astype(BF16)
    decx = ex[CHUNK:CHUNK + 1]
    for g in range(N_SSM_GROUPS):
        cols = slice(g * GROUP_W, (g + 1) * GROUP_W)
        upd = _dot_tn(b_ref[0, :, g * D_STATE:(g + 1) * D_STATE], xw[:, cols])
        h_scr[:, cols] = h_scr[:, cols] * decx[:, cols] + upd


def _ssd_bwd_call(xs, bm, dt, alog, tril, eb):
    b, s, _ = xs.shape
    nc = s // CHUNK
    rev = lambda i, c: (i, nc - 1 - c, 0)
    return pl.pallas_call(
        _ssd_bwd_kernel,
        grid=(b, nc),
        in_specs=[
            pl.BlockSpec((1, CHUNK, D_INNER), rev),
            pl.BlockSpec((1, CHUNK, BC_W), rev),
            pl.BlockSpec((1, CHUNK, LANES), rev),
            pl.BlockSpec(alog.shape, lambda i, c: (0, 0)),
            pl.BlockSpec(tril.shape, lambda i, c: (0, 0)),
            pl.BlockSpec(eb.shape, lambda i, c: (0, 0)),
        ],
        out_specs=pl.BlockSpec((1, 1, D_STATE, D_INNER), lambda i, c: (i, nc - 1 - c, 0, 0)),
        out_shape=jax.ShapeDtypeStruct((b, nc, D_STATE, D_INNER), BF16),
        scratch_shapes=[pltpu.VMEM((D_STATE, D_INNER), F32)],
        compiler_params=_cparams(("parallel", "arbitrary")),
        name="ssd_bwd_states",
    )(xs, bm, dt, alog, tril, eb)


def _ssd_fwd_kernel(x_ref, b_ref, c_ref, dt_ref, z_ref, hb_ref, alog_ref, dskip_ref, nw_ref,
                    tril_ref, ef_ref, eb_ref, y_ref, h_scr):
    c = pl.program_id(1)

    @pl.when(c == 0)
    def _():
        h_scr[...] = jnp.zeros_like(h_scr)

    half = N_SSM_HEADS
    dt = dt_ref[0]
    lane = lax.broadcasted_iota(jnp.int32, (CHUNK, LANES), 1)
    da, pre, tot = _chunk_decay_terms(dt, alog_ref, tril_ref)
    acs = jnp.where(lane < half, pre, tot - pre + da)
    acsp_t = (acs - jnp.log(dt)).T

    row = lax.broadcasted_iota(jnp.int32, (CHUNK, CHUNK), 0)
    col = lax.broadcasted_iota(jnp.int32, (CHUNK, CHUNK), 1)
    lower = col <= row
    eye = col == row

    xb = x_ref[0]
    bm = b_ref[0]
    cm = c_ref[0]
    hb = hb_ref[0, 0]
    lane_x = lax.broadcasted_iota(jnp.int32, (CHUNK, LANES), 1)
    first_head = lane_x < SSM_HEAD_DIM

    y_diag = []
    y_off_f = []
    y_off_b = []
    cb_diag = []
    for g in range(N_SSM_GROUPS):
        bg = bm[:, g * D_STATE:(g + 1) * D_STATE]
        cg = cm[:, g * D_STATE:(g + 1) * D_STATE]
        cb = _dot_nt(cg, bg)
        cb_diag.append(jnp.sum(jnp.where(eye, cb, 0.0), axis=-1, keepdims=True))
        for k in range(HEADS_PER_GROUP // 2):
            ms = []
            for h in (g * HEADS_PER_GROUP + 2 * k, g * HEADS_PER_GROUP + 2 * k + 1):
                arg = jnp.where(lower,
                                acs[:, h:h + 1] - acsp_t[h:h + 1, :],
                                acs[:, half + h:half + h + 1] - acsp_t[half + h:half + h + 1, :])
                ms.append((cb * jnp.exp(arg)).astype(BF16))
            pair = g * (HEADS_PER_GROUP // 2) + k
            xp = xb[:, pair * LANES:(pair + 1) * LANES]
            zero = jnp.zeros_like(xp)
            rhs = jnp.concatenate([jnp.where(first_head, xp, zero), jnp.where(first_head, zero, xp)], axis=0)
            y_diag.append(_dot(jnp.concatenate(ms, axis=1), rhs))
        cols = slice(g * GROUP_W, (g + 1) * GROUP_W)
        y_off_f.append(_dot(cg, h_scr[:, cols].astype(BF16)))
        y_off_b.append(_dot(cg, hb[:, cols]))
    y_diag = jnp.concatenate(y_diag, axis=1)
    y_off_f = jnp.concatenate(y_off_f, axis=1)
    y_off_b = jnp.concatenate(y_off_b, axis=1)

    grp = (lane - half) >> 3
    diag_h = jnp.zeros((CHUNK, LANES), F32)
    for g in range(N_SSM_GROUPS):
        diag_h = jnp.where(grp == g, cb_diag[g], diag_h)
    coef = dskip_ref[...] + diag_h * dt
    scale = jnp.exp(acs)
    wgt = jnp.exp(tot - pre) * dt
    dec = jnp.exp(jnp.broadcast_to(tot, (BF16_ROWS, LANES)))
    ef = ef_ref[...]
    eb = eb_ref[...]
    pk_scale = _pack_hilo(scale, lane)
    scale_f = _dot(pk_scale, ef)
    scale_b = _dot(pk_scale, eb)
    coef_x = _dot(_pack_hilo(coef, lane), eb)
    pk_wd = _pack_hilo(jnp.concatenate([wgt, dec], axis=0),
                       lax.broadcasted_iota(jnp.int32, (CHUNK + BF16_ROWS, LANES), 1))
    wd_x = _dot(pk_wd, ef)

    xf = xb.astype(F32)
    y = y_diag + y_off_f * scale_f + y_off_b * scale_b + coef_x * xf
    y = y * z_ref[0].astype(F32)
    for g in range(N_SSM_GROUPS):
        cols = slice(g * GROUP_W, (g + 1) * GROUP_W)
        yg = y[:, cols]
        ms = jnp.mean(yg * yg, axis=-1, keepdims=True)
        y_ref[0, :, cols] = (yg * lax.rsqrt(ms + RMS_EPS) * nw_ref[:, cols]).astype(BF16)

    xw = (xf * wd_x[:CHUNK]).astype(BF16)
    decx = wd_x[CHUNK:CHUNK + 1]
    for g in range(N_SSM_GROUPS):
        cols = slice(g * GROUP_W, (g + 1) * GROUP_W)
        upd = _dot_tn(bm[:, g * D_STATE:(g + 1) * D_STATE], xw[:, cols])
        h_scr[:, cols] = h_scr[:, cols] * decx[:, cols] + upd


def _ssd_fwd_call(xs, bm, cm, dt, zsil, hb, alog, dskip, nw, tril, ef, eb):
    b, s, _ = xs.shape
    nc = s // CHUNK
    fwd = lambda i, c: (i, c, 0)
    const = lambda i, c: (0, 0)
    return pl.pallas_call(
        _ssd_fwd_kernel,
        grid=(b, nc),
        in_specs=[
            pl.BlockSpec((1, CHUNK, D_INNER), fwd),
            pl.BlockSpec((1, CHUNK, BC_W), fwd),
            pl.BlockSpec((1, CHUNK, BC_W), fwd),
            pl.BlockSpec((1, CHUNK, LANES), fwd),
            pl.BlockSpec((1, CHUNK, D_INNER), fwd),
            pl.BlockSpec((1, 1, D_STATE, D_INNER), lambda i, c: (i, c, 0, 0)),
            pl.BlockSpec(alog.shape, const),
            pl.BlockSpec(dskip.shape, const),
            pl.BlockSpec(nw.shape, const),
            pl.BlockSpec(tril.shape, const),
            pl.BlockSpec(ef.shape, const),
            pl.BlockSpec(eb.shape, const),
        ],
        out_specs=pl.BlockSpec((1, CHUNK, D_INNER), fwd),
        out_shape=jax.ShapeDtypeStruct((b, s, D_INNER), BF16),
        scratch_shapes=[pltpu.VMEM((D_STATE, D_INNER), F32)],
        compiler_params=_cparams(("parallel", "arbitrary")),
        name="ssd_fwd",
    )(xs, bm, cm, dt, zsil, hb, alog, dskip, nw, tril, ef, eb)


def _out_kernel(x_ref, att_ref, y_ref, gate_ref, wa_ref, ws_ref, wo_ref, lg_ref, lb_ref, o_ref):
    att = _dot(att_ref[0], wa_ref[...])
    ssm = _dot(y_ref[0], ws_ref[...])
    ga = gate_ref[0, :, :D_MODEL].astype(F32)
    gs = gate_ref[0, :, D_MODEL:].astype(F32)
    mixed = ga * att + gs * ssm
    out = _dot(mixed.astype(BF16), wo_ref[...])
    r = ALPHA * x_ref[0] + out
    mu = jnp.mean(r, axis=-1, keepdims=True)
    d = r - mu
    var = jnp.mean(d * d, axis=-1, keepdims=True)
    o_ref[0] = d * lax.rsqrt(var + LN_EPS) * lg_ref[...] + lb_ref[...]


def _out_call(x, att, y, gates, wa, ws, wo, lg, lb, tm):
    b, s, _ = x.shape
    tok = lambda i, j: (i, j, 0)
    const = lambda i, j: (0, 0)
    return pl.pallas_call(
        _out_kernel,
        grid=(b, s // tm),
        in_specs=[
            pl.BlockSpec((1, tm, D_MODEL), tok),
            pl.BlockSpec((1, tm, ATT_W), tok),
            pl.BlockSpec((1, tm, D_INNER), tok),
            pl.BlockSpec((1, tm, 2 * D_MODEL), tok),
            pl.BlockSpec(wa.shape, const),
            pl.BlockSpec(ws.shape, const),
            pl.BlockSpec(wo.shape, const),
            pl.BlockSpec(lg.shape, const),
            pl.BlockSpec(lb.shape, const),
        ],
        out_specs=pl.BlockSpec((1, tm, D_MODEL), tok),
        out_shape=jax.ShapeDtypeStruct((b, s, D_MODEL), F32),
        compiler_params=_cparams(("parallel", "parallel")),
        name="merge_out_ln",
    )(x, att, y, gates, wa, ws, wo, lg, lb)


def _rope_tables(seq_len):
    rows = seq_len // GRID_W
    row = jnp.repeat(jnp.arange(rows, dtype=F32), GRID_W)
    col = jnp.tile(jnp.arange(GRID_W, dtype=F32), rows)
    freqs = ROPE_THETA ** (-jnp.arange(0, AXIS_DIM, 2, dtype=F32) / AXIS_DIM)
    ang = jnp.concatenate([row[:, None] * freqs, col[:, None] * freqs], axis=-1)
    cos, sin = jnp.cos(ang), jnp.sin(ang)
    reps = LANES // HEAD_DIM
    cos_t = jnp.tile(jnp.concatenate([cos, cos], axis=-1), (1, reps))
    sin_t = jnp.tile(jnp.concatenate([-sin, sin], axis=-1), (1, reps))
    return cos_t, sin_t


def _expand_matrix(lane_offset):
    r = np.arange(LANES)[:, None] % (LANES // 2)
    head = np.arange(D_INNER)[None, :] // SSM_HEAD_DIM
    return jnp.asarray((r == head + lane_offset).astype(np.float32), dtype=BF16)


def _layer(x, p, tables):
    b, s, _ = x.shape
    tm = min(512, s)
    tq = min(256, s)
    cos_t, sin_t = tables
    q, kt, v = _qkv_call(x, p["w_qkv"], cos_t[:s], sin_t[:s], p["qw"], p["kw"], p["bd"], tm)
    gsil, zsil = _gz_call(x, p["w_gz"], tm)
    xs, bm, cm = _xbc_call(x, p["w_xbc"], p["conv_w"], p["conv_b"], tm)
    gates, dt = _gdt_call(x, p["w_gdt"], p["b_gate"], p["b_dt"], tm)
    att = _attn_call(q, kt, v, gsil, tq)
    hb = _ssd_bwd_call(xs, bm, dt, p["alog"], p["tril"], p["eb"])
    y = _ssd_fwd_call(xs, bm, cm, dt, zsil, hb, p["alog"], p["dskip"], p["nw"], p["tril"], p["ef"], p["eb"])
    return _out_call(x, att, y, gates, p["wa"], p["ws"], p["wo"], p["lg"], p["lb"], tm)


def _prepare(w_in, b_gate, q_norm_w, k_norm_w, conv_w, conv_b, dt_bias_fwd, dt_bias_bwd,
             a_log_fwd, a_log_bwd, d_skip, ssm_norm_w, w_att_proj, w_ssm_proj, w_out, ln_g, ln_b):
    wq, wk, wv, wg, wz, wxbc, wdt, wgate = jnp.split(w_in, SPLIT_POINTS, axis=-1)
    perm = np.concatenate([np.arange(0, HEAD_DIM, 2), np.arange(1, HEAD_DIM, 2)])
    qperm = (np.arange(N_Q_HEADS)[:, None] * HEAD_DIM + perm[None, :]).reshape(-1)
    kperm = (np.arange(N_KV_HEADS)[:, None] * HEAD_DIM + perm[None, :]).reshape(-1)
    pad = LANES - 2 * N_SSM_HEADS
    zeros_h = jnp.zeros((N_SSM_HEADS,), F32)
    zeros_p = jnp.zeros((pad,), F32)
    bd = np.kron(np.eye(2 * LANES // HEAD_DIM), np.full((HEAD_DIM, HEAD_DIM), 1.0 / HEAD_DIM))
    return {
        "w_qkv": jnp.concatenate([wq[:, qperm], wk[:, kperm], wv], axis=1).astype(BF16),
        "qw": jnp.tile(q_norm_w[perm], 2 * LANES // HEAD_DIM)[None, :],
        "kw": jnp.tile(k_norm_w[perm], 2 * LANES // HEAD_DIM)[None, :],
        "bd": jnp.asarray(bd, dtype=BF16),
        "w_gz": jnp.concatenate([wg, wz], axis=1).astype(BF16),
        "w_xbc": wxbc.astype(BF16),
        "conv_w": conv_w,
        "conv_b": conv_b[None, :],
        "w_gdt": jnp.concatenate([wgate, wdt, jnp.zeros((D_MODEL, pad), F32)], axis=1).astype(BF16),
        "b_gate": b_gate[None, :],
        "b_dt": jnp.concatenate([dt_bias_fwd, dt_bias_bwd, zeros_p])[None, :],
        "alog": jnp.concatenate([a_log_fwd, a_log_bwd, zeros_p])[None, :],
        "dskip": jnp.concatenate([zeros_h, d_skip, zeros_p])[None, :],
        "nw": ssm_norm_w[None, :],
        "tril": jnp.asarray(np.tril(np.ones((CHUNK, CHUNK), np.float32)), dtype=BF16),
        "ef": _expand_matrix(0),
        "eb": _expand_matrix(N_SSM_HEADS),
        "wa": w_att_proj.astype(BF16),
        "ws": w_ssm_proj.astype(BF16),
        "wo": w_out.astype(BF16),
        "lg": ln_g[None, :],
        "lb": ln_b[None, :],
    }


def kernel(x_prompt, x_sample, w_in, b_gate, q_norm_w, k_norm_w, conv_w, conv_b, dt_bias_fwd,
           dt_bias_bwd, a_log_fwd, a_log_bwd, d_skip, ssm_norm_w, w_att_proj, w_ssm_proj, w_out,
           ln_g, ln_b):
    xp, xs = x_prompt, x_sample
    tables = _rope_tables(max(xp.shape[1], xs.shape[1]))
    for l in range(w_in.shape[0]):
        p = _prepare(w_in[l], b_gate[l], q_norm_w[l], k_norm_w[l], conv_w[l], conv_b[l],
                     dt_bias_fwd[l], dt_bias_bwd[l], a_log_fwd[l], a_log_bwd[l], d_skip[l],
                     ssm_norm_w[l], w_att_proj[l], w_ssm_proj[l], w_out[l], ln_g[l], ln_b[l])
        xp = _layer(xp, p, tables)
        xs = _layer(xs, p, tables)
    return (xp, xs)
```

```python
import numpy as np
import jax
import jax.numpy as jnp
from jax import lax
from jax.experimental import pallas as pl
from jax.experimental.pallas import tpu as pltpu

F32 = jnp.float32
BF16 = jnp.bfloat16

D_MODEL = 1024
GRID_W = 64
N_Q_HEADS = 16
N_KV_HEADS = 4
HEAD_DIM = 64
ATT_W = N_Q_HEADS * HEAD_DIM
KV_W = N_KV_HEADS * HEAD_DIM
AXIS_DIM = HEAD_DIM // 2
ROPE_THETA = 10000.0
D_INNER = 2 * D_MODEL
SSM_HEAD_DIM = 64
N_SSM_HEADS = D_INNER // SSM_HEAD_DIM
N_SSM_GROUPS = 4
D_STATE = 128
CONV_W = 5
CHUNK = 128
BC_W = N_SSM_GROUPS * D_STATE
CONV_CH = D_INNER + 2 * BC_W
RMS_EPS = 1e-6
LN_EPS = 1e-5
DEPTH = 1
ALPHA = (2 * DEPTH) ** 0.25
SPLITS = (ATT_W, KV_W, KV_W, ATT_W, D_INNER, CONV_CH, 2 * N_SSM_HEADS, 2 * D_MODEL)
SPLIT_POINTS = tuple(int(v) for v in np.cumsum(SPLITS)[:-1])

LANES = 128
HALO = 8
BF16_ROWS = 16
GROUP_W = D_INNER // N_SSM_GROUPS
HEADS_PER_GROUP = N_SSM_HEADS // N_SSM_GROUPS
VMEM_LIMIT = 48 * 1024 * 1024
MAX_TOKEN_TILE = 512
MAX_QUERY_TILE = 256
ATTN_KEY_BLOCK = 256
Q_SCALE = float(np.log2(np.e)) * HEAD_DIM ** -0.5


def _cparams(sem):
    return pltpu.CompilerParams(dimension_semantics=sem, vmem_limit_bytes=VMEM_LIMIT)


def _dot(a, b):
    return jnp.dot(a, b, preferred_element_type=F32)


def _dot_nt(a, b):
    return lax.dot_general(a, b, (((1,), (1,)), ((), ())), preferred_element_type=F32)


def _dot_tn(a, b):
    return lax.dot_general(a, b, (((0,), (0,)), ((), ())), preferred_element_type=F32)


def _silu(x):
    return x / (1.0 + jnp.exp(-x))


def _split2(v):
    hi = v.astype(BF16)
    lo = (v - hi.astype(F32)).astype(BF16)
    return hi, lo


def _split3(v):
    hi = v.astype(BF16)
    r = v - hi.astype(F32)
    mid = r.astype(BF16)
    lo = (r - mid.astype(F32)).astype(BF16)
    return hi, mid, lo


def _qkv_kernel(x_ref, wt_ref, cos_ref, sin_ref, qw_ref, kw_ref, qt_ref, k_ref, vt_ref):
    xb = x_ref[0].astype(BF16)
    cosv = cos_ref[...]
    sinv = sin_ref[...]
    half = HEAD_DIM // 2

    def norm_rope(acc, w_ref, scale):
        x0, x1 = acc[:half], acc[half:]
        ms = (jnp.sum(x0 * x0, axis=0, keepdims=True) + jnp.sum(x1 * x1, axis=0, keepdims=True)) * (1.0 / HEAD_DIM)
        inv = lax.rsqrt(ms + RMS_EPS)
        a0 = x0 * inv * w_ref[:half]
        a1 = x1 * inv * w_ref[half:]
        return (a0 * cosv - a1 * sinv) * scale, (a0 * sinv + a1 * cosv) * scale

    rows = 4 * HEAD_DIM
    for c in range(ATT_W // rows):
        acc = _dot_nt(wt_ref[c * rows:(c + 1) * rows, :], xb)
        for h in range(rows // HEAD_DIM):
            o0, o1 = norm_rope(acc[h * HEAD_DIM:(h + 1) * HEAD_DIM], qw_ref, Q_SCALE)
            r0 = c * rows + h * HEAD_DIM
            qt_ref[0, r0:r0 + half, :] = o0.astype(BF16)
            qt_ref[0, r0 + half:r0 + HEAD_DIM, :] = o1.astype(BF16)
    acc = _dot_nt(wt_ref[ATT_W:ATT_W + KV_W, :], xb)
    for g in range(N_KV_HEADS):
        o0, o1 = norm_rope(acc[g * HEAD_DIM:(g + 1) * HEAD_DIM], kw_ref, 1.0)
        k_ref[0, g] = jnp.concatenate([o0, o1], axis=0).T.astype(BF16)
    vt_ref[0] = _dot_nt(wt_ref[ATT_W + KV_W:ATT_W + 2 * KV_W, :], xb).astype(BF16)


def _qkv_call(x, wt_qkv, cos_t, sin_t, qw, kw, tm):
    b, s, _ = x.shape
    half = HEAD_DIM // 2
    return pl.pallas_call(
        _qkv_kernel,
        grid=(b, s // tm),
        in_specs=[
            pl.BlockSpec((1, tm, D_MODEL), lambda i, j: (i, j, 0)),
            pl.BlockSpec(wt_qkv.shape, lambda i, j: (0, 0)),
            pl.BlockSpec((half, tm), lambda i, j: (0, j)),
            pl.BlockSpec((half, tm), lambda i, j: (0, j)),
            pl.BlockSpec(qw.shape, lambda i, j: (0, 0)),
            pl.BlockSpec(kw.shape, lambda i, j: (0, 0)),
        ],
        out_specs=[
            pl.BlockSpec((1, ATT_W, tm), lambda i, j: (i, 0, j)),
            pl.BlockSpec((1, N_KV_HEADS, tm, HEAD_DIM), lambda i, j: (i, 0, j, 0)),
            pl.BlockSpec((1, KV_W, tm), lambda i, j: (i, 0, j)),
        ],
        out_shape=[
            jax.ShapeDtypeStruct((b, ATT_W, s), BF16),
            jax.ShapeDtypeStruct((b, N_KV_HEADS, s, HEAD_DIM), BF16),
            jax.ShapeDtypeStruct((b, KV_W, s), BF16),
        ],
        compiler_params=_cparams(("parallel", "parallel")),
        name="qkv_proj",
    )(x, wt_qkv, cos_t, sin_t, qw, kw)


def _gz_kernel(x_ref, w_ref, g_ref, z_ref):
    xb = x_ref[0].astype(BF16)
    cw = 4 * LANES
    for c in range(ATT_W // cw):
        acc = _dot(xb, w_ref[:, c * cw:(c + 1) * cw])
        g_ref[0, :, c * cw:(c + 1) * cw] = _silu(acc).astype(BF16)
    for c in range(D_INNER // cw):
        acc = _dot(xb, w_ref[:, ATT_W + c * cw:ATT_W + (c + 1) * cw])
        z_ref[0, :, c * cw:(c + 1) * cw] = _silu(acc).astype(BF16)


def _gz_call(x, w_gz, tm):
    b, s, _ = x.shape
    return pl.pallas_call(
        _gz_kernel,
        grid=(b, s // tm),
        in_specs=[
            pl.BlockSpec((1, tm, D_MODEL), lambda i, j: (i, j, 0)),
            pl.BlockSpec(w_gz.shape, lambda i, j: (0, 0)),
        ],
        out_specs=[
            pl.BlockSpec((1, tm, ATT_W), lambda i, j: (i, j, 0)),
            pl.BlockSpec((1, tm, D_INNER), lambda i, j: (i, j, 0)),
        ],
        out_shape=[
            jax.ShapeDtypeStruct((b, s, ATT_W), BF16),
            jax.ShapeDtypeStruct((b, s, D_INNER), BF16),
        ],
        compiler_params=_cparams(("parallel", "parallel")),
        name="gate_proj",
    )(x, w_gz)


def _xbc_kernel(x_ref, xp_ref, xn_ref, w_ref, cw_ref, cb_ref, xs_ref, b_ref, c_ref, acc_ref):
    tm = x_ref.shape[1]
    j = pl.program_id(1)
    nj = pl.num_programs(1)
    prev = jnp.where(j > 0, xp_ref[0], 0.0)
    nxt = jnp.where(j < nj - 1, xn_ref[0], 0.0)
    xall = jnp.concatenate([prev, x_ref[0], nxt], axis=0).astype(BF16)
    cw = 4 * LANES
    outs = ([(xs_ref, c * cw) for c in range(D_INNER // cw)] + [(b_ref, 0), (c_ref, 0)])
    for c, (o_ref, off) in enumerate(outs):
        acc_ref[...] = _dot(xall, w_ref[:, c * cw:(c + 1) * cw])
        y = cb_ref[:, c * cw:(c + 1) * cw]
        for k in range(CONV_W):
            r0 = HALO - CONV_W // 2 + k
            y = y + acc_ref[r0:r0 + tm, :] * cw_ref[k:k + 1, c * cw:(c + 1) * cw]
        o_ref[0, :, off:off + cw] = _silu(y).astype(BF16)


def _xbc_call(x, w_xbc, conv_w, conv_b, tm):
    b, s, _ = x.shape
    hb = tm // HALO
    last = s // HALO - 1
    cw = 4 * LANES
    return pl.pallas_call(
        _xbc_kernel,
        grid=(b, s // tm),
        in_specs=[
            pl.BlockSpec((1, tm, D_MODEL), lambda i, j: (i, j, 0)),
            pl.BlockSpec((1, HALO, D_MODEL), lambda i, j: (i, jnp.maximum(j * hb - 1, 0), 0)),
            pl.BlockSpec((1, HALO, D_MODEL), lambda i, j: (i, jnp.minimum((j + 1) * hb, last), 0)),
            pl.BlockSpec(w_xbc.shape, lambda i, j: (0, 0)),
            pl.BlockSpec(conv_w.shape, lambda i, j: (0, 0)),
            pl.BlockSpec(conv_b.shape, lambda i, j: (0, 0)),
        ],
        out_specs=[
            pl.BlockSpec((1, tm, D_INNER), lambda i, j: (i, j, 0)),
            pl.BlockSpec((1, tm, BC_W), lambda i, j: (i, j, 0)),
            pl.BlockSpec((1, tm, BC_W), lambda i, j: (i, j, 0)),
        ],
        out_shape=[
            jax.ShapeDtypeStruct((b, s, D_INNER), BF16),
            jax.ShapeDtypeStruct((b, s, BC_W), BF16),
            jax.ShapeDtypeStruct((b, s, BC_W), BF16),
        ],
        scratch_shapes=[pltpu.VMEM((tm + 2 * HALO, cw), F32)],
        compiler_params=_cparams(("parallel", "parallel")),
        name="xbc_conv_proj",
    )(x, x, x, w_xbc, conv_w, conv_b)


def _gdt_kernel(x_ref, w_ref, bg_ref, bdt_ref, gate_ref, dt_ref):
    xb = x_ref[0].astype(BF16)
    cw = 4 * LANES
    for c in range(2 * D_MODEL // cw):
        acc = _dot(xb, w_ref[:, c * cw:(c + 1) * cw]) + bg_ref[:, c * cw:(c + 1) * cw]
        gate_ref[0, :, c * cw:(c + 1) * cw] = (1.0 / (1.0 + jnp.exp(-acc))).astype(BF16)
    v = _dot(xb, w_ref[:, 2 * D_MODEL:2 * D_MODEL + LANES]) + bdt_ref[...]
    dt_ref[0] = jnp.maximum(v, 0.0) + jnp.log1p(jnp.exp(-jnp.abs(v)))


def _gdt_call(x, w_gdt, b_gate, b_dt, tm):
    b, s, _ = x.shape
    return pl.pallas_call(
        _gdt_kernel,
        grid=(b, s // tm),
        in_specs=[
            pl.BlockSpec((1, tm, D_MODEL), lambda i, j: (i, j, 0)),
            pl.BlockSpec(w_gdt.shape, lambda i, j: (0, 0)),
            pl.BlockSpec(b_gate.shape, lambda i, j: (0, 0)),
            pl.BlockSpec(b_dt.shape, lambda i, j: (0, 0)),
        ],
        out_specs=[
            pl.BlockSpec((1, tm, 2 * D_MODEL), lambda i, j: (i, j, 0)),
            pl.BlockSpec((1, tm, LANES), lambda i, j: (i, j, 0)),
        ],
        out_shape=[
            jax.ShapeDtypeStruct((b, s, 2 * D_MODEL), BF16),
            jax.ShapeDtypeStruct((b, s, LANES), F32),
        ],
        compiler_params=_cparams(("parallel", "parallel")),
        name="gate_dt_proj",
    )(x, w_gdt, b_gate, b_dt)


def _attn_kernel(qt_ref, k_ref, vt_ref, g_ref, o_ref):
    k = k_ref[0, 0]
    vt = vt_ref[0]
    vt_aug = jnp.concatenate([vt, jnp.ones((BF16_ROWS, vt.shape[1]), BF16)], axis=0)
    rep = N_Q_HEADS // N_KV_HEADS
    tq = qt_ref.shape[2]
    s_len = k.shape[0]
    kb = min(ATTN_KEY_BLOCK, s_len)
    nblk = s_len // kb

    qt = jnp.concatenate([qt_ref[0, r * HEAD_DIM:(r + 1) * HEAD_DIM, :] for r in range(rep)], axis=1)
    m = acc = None
    st_next = _dot(k[:kb], qt)
    for j in range(nblk):
        st = st_next
        if j + 1 < nblk:
            st_next = _dot(k[(j + 1) * kb:(j + 2) * kb], qt)
        bm = jnp.max(st, axis=0, keepdims=True)
        m_new = bm if m is None else jnp.maximum(m, bm)
        pt = jnp.exp2(st - m_new).astype(BF16)
        part = _dot(vt_aug[:, j * kb:(j + 1) * kb], pt)
        acc = part if acc is None else acc * jnp.exp2(m - m_new) + part
        m = m_new
    on = acc[:HEAD_DIM] / acc[HEAD_DIM:HEAD_DIM + 1]
    o = jnp.concatenate([on[:, r * tq:(r + 1) * tq].T for r in range(rep)], axis=1)
    o_ref[0] = (o * g_ref[0].astype(F32)).astype(BF16)


def _attn_call(qt, k, vt, gsil, tq):
    b, _, s = qt.shape
    gw = ATT_W // N_KV_HEADS
    return pl.pallas_call(
        _attn_kernel,
        grid=(b, N_KV_HEADS, s // tq),
        in_specs=[
            pl.BlockSpec((1, gw, tq), lambda i, g, j: (i, g, j)),
            pl.BlockSpec((1, 1, s, HEAD_DIM), lambda i, g, j: (i, g, 0, 0)),
            pl.BlockSpec((1, HEAD_DIM, s), lambda i, g, j: (i, g, 0)),
            pl.BlockSpec((1, tq, gw), lambda i, g, j: (i, j, g)),
        ],
        out_specs=pl.BlockSpec((1, tq, gw), lambda i, g, j: (i, j, g)),
        out_shape=jax.ShapeDtypeStruct((b, s, ATT_W), BF16),
        compiler_params=_cparams(("parallel", "parallel", "parallel")),
        name="attention",
    )(qt, k, vt, gsil)


def _pack_hilo(v, lane):
    hi = v.astype(BF16).astype(F32)
    lo = (v - hi).astype(BF16).astype(F32)
    return jnp.where(lane < LANES // 2, hi, pltpu.roll(lo, LANES // 2, 1)).astype(BF16)


def _chunk_decay_terms(dt, alog_ref, tril_ref):
    a = -jnp.exp(alog_ref[...])
    da = dt * a
    tril = tril_ref[...]
    hi, mid, lo = _split3(da)
    pre = _dot(tril, hi) + _dot(tril, mid) + _dot(tril, lo)
    tot = pre[CHUNK - 1:CHUNK, :]
    return da, pre, tot


def _ssd_bwd_kernel(x_ref, b_ref, dt_ref, alog_ref, tril_ref, eb_ref, hb_ref, h_scr):
    c = pl.program_id(1)

    @pl.when(c == 0)
    def _():
        h_scr[...] = jnp.zeros_like(h_scr)

    hb_ref[0, 0] = h_scr[...].astype(BF16)
    dt = dt_ref[0]
    lane = lax.broadcasted_iota(jnp.int32, (CHUNK, LANES), 1)
    da, pre, tot = _chunk_decay_terms(dt, alog_ref, tril_ref)
    wgt = jnp.exp(pre - da) * dt
    dec = jnp.exp(jnp.broadcast_to(tot, (BF16_ROWS, LANES)))
    packed = _pack_hilo(jnp.concatenate([wgt, dec], axis=0),
                        lax.broadcasted_iota(jnp.int32, (CHUNK + BF16_ROWS, LANES), 1))
    ex = _dot(packed, eb_ref[...])
    xw = (x_ref[0].astype(F32) * ex[:CHUNK]).astype(BF16)
    decx = ex[CHUNK:CHUNK + 1]
    for g in range(N_SSM_GROUPS):
        cols = slice(g * GROUP_W, (g + 1) * GROUP_W)
        upd = _dot_tn(b_ref[0, :, g * D_STATE:(g + 1) * D_STATE], xw[:, cols])
        h_scr[:, cols] = h_scr[:, cols] * decx[:, cols] + upd


def _ssd_bwd_call(xs, bm, dt, alog, tril, eb):
    b, s, _ = xs.shape
    nc = s // CHUNK
    rev = lambda i, c: (i, nc - 1 - c, 0)
    return pl.pallas_call(
        _ssd_bwd_kernel,
        grid=(b, nc),
        in_specs=[
            pl.BlockSpec((1, CHUNK, D_INNER), rev),
            pl.BlockSpec((1, CHUNK, BC_W), rev),
            pl.BlockSpec((1, CHUNK, LANES), rev),
            pl.BlockSpec(alog.shape, lambda i, c: (0, 0)),
            pl.BlockSpec(tril.shape, lambda i, c: (0, 0)),
            pl.BlockSpec(eb.shape, lambda i, c: (0, 0)),
        ],
        out_specs=pl.BlockSpec((1, 1, D_STATE, D_INNER), lambda i, c: (i, nc - 1 - c, 0, 0)),
        out_shape=jax.ShapeDtypeStruct((b, nc, D_STATE, D_INNER), BF16),
        scratch_shapes=[pltpu.VMEM((D_STATE, D_INNER), F32)],
        compiler_params=_cparams(("parallel", "arbitrary")),
        name="ssd_bwd_states",
    )(xs, bm, dt, alog, tril, eb)


def _ssd_fwd_kernel(x_ref, b_ref, c_ref, dt_ref, z_ref, hb_ref, alog_ref, dskip_ref, nw_ref,
                    tril_ref, ef_ref, eb_ref, y_ref, h_scr):
    c = pl.program_id(1)

    @pl.when(c == 0)
    def _():
        h_scr[...] = jnp.zeros_like(h_scr)

    half = N_SSM_HEADS
    dt = dt_ref[0]
    lane = lax.broadcasted_iota(jnp.int32, (CHUNK, LANES), 1)
    da, pre, tot = _chunk_decay_terms(dt, alog_ref, tril_ref)
    acs = jnp.where(lane < half, pre, tot - pre + da)
    acsp_t = (acs - jnp.log(dt)).T

    row = lax.broadcasted_iota(jnp.int32, (CHUNK, CHUNK), 0)
    col = lax.broadcasted_iota(jnp.int32, (CHUNK, CHUNK), 1)
    lower = col <= row
    eye = col == row

    xb = x_ref[0]
    bm = b_ref[0]
    cm = c_ref[0]
    hb = hb_ref[0, 0]
    lane_x = lax.broadcasted_iota(jnp.int32, (CHUNK, LANES), 1)
    first_head = lane_x < SSM_HEAD_DIM

    y_diag = []
    y_off_f = []
    y_off_b = []
    cb_diag = []
    for g in range(N_SSM_GROUPS):
        bg = bm[:, g * D_STATE:(g + 1) * D_STATE]
        cg = cm[:, g * D_STATE:(g + 1) * D_STATE]
        cb = _dot_nt(cg, bg)
        cb_diag.append(jnp.sum(jnp.where(eye, cb, 0.0), axis=-1, keepdims=True))
        for k in range(HEADS_PER_GROUP // 2):
            ms = []
            for h in (g * HEADS_PER_GROUP + 2 * k, g * HEADS_PER_GROUP + 2 * k + 1):
                arg = jnp.where(lower,
                                acs[:, h:h + 1] - acsp_t[h:h + 1, :],
                                acs[:, half + h:half + h + 1] - acsp_t[half + h:half + h + 1, :])
                ms.append((cb * jnp.exp(arg)).astype(BF16))
            pair = g * (HEADS_PER_GROUP // 2) + k
            xp = xb[:, pair * LANES:(pair + 1) * LANES]
            zero = jnp.zeros_like(xp)
            rhs = jnp.concatenate([jnp.where(first_head, xp, zero), jnp.where(first_head, zero, xp)], axis=0)
            y_diag.append(_dot(jnp.concatenate(ms, axis=1), rhs))
        cols = slice(g * GROUP_W, (g + 1) * GROUP_W)
        y_off_f.append(_dot(cg, h_scr[:, cols].astype(BF16)))
        y_off_b.append(_dot(cg, hb[:, cols]))
    y_diag = jnp.concatenate(y_diag, axis=1)
    y_off_f = jnp.concatenate(y_off_f, axis=1)
    y_off_b = jnp.concatenate(y_off_b, axis=1)

    grp = (lane - half) >> 3
    diag_h = jnp.zeros((CHUNK, LANES), F32)
    for g in range(N_SSM_GROUPS):
        diag_h = jnp.where(grp == g, cb_diag[g], diag_h)
    coef = dskip_ref[...] + diag_h * dt
    scale = jnp.exp(acs)
    wgt = jnp.exp(tot - pre) * dt
    dec = jnp.exp(jnp.broadcast_to(tot, (BF16_ROWS, LANES)))
    ef = ef_ref[...]
    eb = eb_ref[...]
    pk_scale = _pack_hilo(scale, lane)
    scale_f = _dot(pk_scale, ef)
    scale_b = _dot(pk_scale, eb)
    coef_x = _dot(_pack_hilo(coef, lane), eb)
    pk_wd = _pack_hilo(jnp.concatenate([wgt, dec], axis=0),
                       lax.broadcasted_iota(jnp.int32, (CHUNK + BF16_ROWS, LANES), 1))
    wd_x = _dot(pk_wd, ef)

    xf = xb.astype(F32)
    y = y_diag + y_off_f * scale_f + y_off_b * scale_b + coef_x * xf
    y = y * z_ref[0].astype(F32)
    for g in range(N_SSM_GROUPS):
        cols = slice(g * GROUP_W, (g + 1) * GROUP_W)
        yg = y[:, cols]
        ms = jnp.mean(yg * yg, axis=-1, keepdims=True)
        y_ref[0, :, cols] = (yg * lax.rsqrt(ms + RMS_EPS) * nw_ref[:, cols]).astype(BF16)

    xw = (xf * wd_x[:CHUNK]).astype(BF16)
    decx = wd_x[CHUNK:CHUNK + 1]
    for g in range(N_SSM_GROUPS):
        cols = slice(g * GROUP_W, (g + 1) * GROUP_W)
        upd = _dot_tn(bm[:, g * D_STATE:(g + 1) * D_STATE], xw[:, cols])
        h_scr[:, cols] = h_scr[:, cols] * decx[:, cols] + upd


def _ssd_fwd_call(xs, bm, cm, dt, zsil, hb, alog, dskip, nw, tril, ef, eb):
    b, s, _ = xs.shape
    nc = s // CHUNK
    fwd = lambda i, c: (i, c, 0)
    const = lambda i, c: (0, 0)
    return pl.pallas_call(
        _ssd_fwd_kernel,
        grid=(b, nc),
        in_specs=[
            pl.BlockSpec((1, CHUNK, D_INNER), fwd),
            pl.BlockSpec((1, CHUNK, BC_W), fwd),
            pl.BlockSpec((1, CHUNK, BC_W), fwd),
            pl.BlockSpec((1, CHUNK, LANES), fwd),
            pl.BlockSpec((1, CHUNK, D_INNER), fwd),
            pl.BlockSpec((1, 1, D_STATE, D_INNER), lambda i, c: (i, c, 0, 0)),
            pl.BlockSpec(alog.shape, const),
            pl.BlockSpec(dskip.shape, const),
            pl.BlockSpec(nw.shape, const),
            pl.BlockSpec(tril.shape, const),
            pl.BlockSpec(ef.shape, const),
            pl.BlockSpec(eb.shape, const),
        ],
        out_specs=pl.BlockSpec((1, CHUNK, D_INNER), fwd),
        out_shape=jax.ShapeDtypeStruct((b, s, D_INNER), BF16),
        scratch_shapes=[pltpu.VMEM((D_STATE, D_INNER), F32)],
        compiler_params=_cparams(("parallel", "arbitrary")),
        name="ssd_fwd",
    )(xs, bm, cm, dt, zsil, hb, alog, dskip, nw, tril, ef, eb)


def _out_kernel(x_ref, att_ref, y_ref, gate_ref, wa_ref, ws_ref, wo_ref, lg_ref, lb_ref, o_ref):
    att = _dot(att_ref[0], wa_ref[...])
    ssm = _dot(y_ref[0], ws_ref[...])
    ga = gate_ref[0, :, :D_MODEL].astype(F32)
    gs = gate_ref[0, :, D_MODEL:].astype(F32)
    mixed = ga * att + gs * ssm
    out = _dot(mixed.astype(BF16), wo_ref[...])
    r = ALPHA * x_ref[0] + out
    mu = jnp.mean(r, axis=-1, keepdims=True)
    d = r - mu
    var = jnp.mean(d * d, axis=-1, keepdims=True)
    o_ref[0] = d * lax.rsqrt(var + LN_EPS) * lg_ref[...] + lb_ref[...]


def _out_call(x, att, y, gates, wa, ws, wo, lg, lb, tm):
    b, s, _ = x.shape
    tok = lambda i, j: (i, j, 0)
    const = lambda i, j: (0, 0)
    return pl.pallas_call(
        _out_kernel,
        grid=(b, s // tm),
        in_specs=[
            pl.BlockSpec((1, tm, D_MODEL), tok),
            pl.BlockSpec((1, tm, ATT_W), tok),
            pl.BlockSpec((1, tm, D_INNER), tok),
            pl.BlockSpec((1, tm, 2 * D_MODEL), tok),
            pl.BlockSpec(wa.shape, const),
            pl.BlockSpec(ws.shape, const),
            pl.BlockSpec(wo.shape, const),
            pl.BlockSpec(lg.shape, const),
            pl.BlockSpec(lb.shape, const),
        ],
        out_specs=pl.BlockSpec((1, tm, D_MODEL), tok),
        out_shape=jax.ShapeDtypeStruct((b, s, D_MODEL), F32),
        compiler_params=_cparams(("parallel", "parallel")),
        name="merge_out_ln",
    )(x, att, y, gates, wa, ws, wo, lg, lb)


def _rope_tables(seq_len):
    rows = seq_len // GRID_W
    row = jnp.repeat(jnp.arange(rows, dtype=F32), GRID_W)
    col = jnp.tile(jnp.arange(GRID_W, dtype=F32), rows)
    freqs = ROPE_THETA ** (-jnp.arange(0, AXIS_DIM, 2, dtype=F32) / AXIS_DIM)
    ang = jnp.concatenate([row[:, None] * freqs, col[:, None] * freqs], axis=-1)
    return jnp.cos(ang).T, jnp.sin(ang).T


def _expand_matrix(lane_offset):
    r = np.arange(LANES)[:, None] % (LANES // 2)
    head = np.arange(D_INNER)[None, :] // SSM_HEAD_DIM
    return jnp.asarray((r == head + lane_offset).astype(np.float32), dtype=BF16)


def _layer(x, p, tables):
    b, s, _ = x.shape
    tm = min(MAX_TOKEN_TILE, s)
    tq = min(MAX_QUERY_TILE, s)
    cos_t, sin_t = tables
    qt, k, vt = _qkv_call(x, p["wt_qkv"], cos_t[:, :s], sin_t[:, :s], p["qw"][:, :tm], p["kw"][:, :tm], tm)
    gsil, zsil = _gz_call(x, p["w_gz"], tm)
    xs, bm, cm = _xbc_call(x, p["w_xbc"], p["conv_w"], p["conv_b"], tm)
    gates, dt = _gdt_call(x, p["w_gdt"], p["b_gate"], p["b_dt"], tm)
    att = _attn_call(qt, k, vt, gsil, tq)
    hb = _ssd_bwd_call(xs, bm, dt, p["alog"], p["tril"], p["eb"])
    y = _ssd_fwd_call(xs, bm, cm, dt, zsil, hb, p["alog"], p["dskip"], p["nw"], p["tril"], p["ef"], p["eb"])
    return _out_call(x, att, y, gates, p["wa"], p["ws"], p["wo"], p["lg"], p["lb"], tm)


def _prepare(w_in, b_gate, q_norm_w, k_norm_w, conv_w, conv_b, dt_bias_fwd, dt_bias_bwd,
             a_log_fwd, a_log_bwd, d_skip, ssm_norm_w, w_att_proj, w_ssm_proj, w_out, ln_g, ln_b):
    wq, wk, wv, wg, wz, wxbc, wdt, wgate = jnp.split(w_in, SPLIT_POINTS, axis=-1)
    perm = np.concatenate([np.arange(0, HEAD_DIM, 2), np.arange(1, HEAD_DIM, 2)])
    qperm = (np.arange(N_Q_HEADS)[:, None] * HEAD_DIM + perm[None, :]).reshape(-1)
    kperm = (np.arange(N_KV_HEADS)[:, None] * HEAD_DIM + perm[None, :]).reshape(-1)
    pad = LANES - 2 * N_SSM_HEADS
    zeros_h = jnp.zeros((N_SSM_HEADS,), F32)
    zeros_p = jnp.zeros((pad,), F32)
    return {
        "wt_qkv": jnp.concatenate([wq[:, qperm], wk[:, kperm], wv], axis=1).T.astype(BF16),
        "qw": jnp.broadcast_to(q_norm_w[perm][:, None], (HEAD_DIM, MAX_TOKEN_TILE)),
        "kw": jnp.broadcast_to(k_norm_w[perm][:, None], (HEAD_DIM, MAX_TOKEN_TILE)),
        "w_gz": jnp.concatenate([wg, wz], axis=1).astype(BF16),
        "w_xbc": wxbc.astype(BF16),
        "conv_w": conv_w,
        "conv_b": conv_b[None, :],
        "w_gdt": jnp.concatenate([wgate, wdt, jnp.zeros((D_MODEL, pad), F32)], axis=1).astype(BF16),
        "b_gate": b_gate[None, :],
        "b_dt": jnp.concatenate([dt_bias_fwd, dt_bias_bwd, zeros_p])[None, :],
        "alog": jnp.concatenate([a_log_fwd, a_log_bwd, zeros_p])[None, :],
        "dskip": jnp.concatenate([zeros_h, d_skip, zeros_p])[None, :],
        "nw": ssm_norm_w[None, :],
        "tril": jnp.asarray(np.tril(np.ones((CHUNK, CHUNK), np.float32)), dtype=BF16),
        "ef": _expand_matrix(0),
        "eb": _expand_matrix(N_SSM_HEADS),
        "wa": w_att_proj.astype(BF16),
        "ws": w_ssm_proj.astype(BF16),
        "wo": w_out.astype(BF16),
        "lg": ln_g[None, :],
        "lb": ln_b[None, :],
    }


def kernel(x_prompt, x_sample, w_in, b_gate, q_norm_w, k_norm_w, conv_w, conv_b, dt_bias_fwd,
           dt_bias_bwd, a_log_fwd, a_log_bwd, d_skip, ssm_norm_w, w_att_proj, w_ssm_proj, w_out,
           ln_g, ln_b):
    xp, xs = x_prompt, x_sample
    tables = _rope_tables(max(xp.shape[1], xs.shape[1]))
    for l in range(w_in.shape[0]):
        p = _prepare(w_in[l], b_gate[l], q_norm_w[l], k_norm_w[l], conv_w[l], conv_b[l],
                     dt_bias_fwd[l], dt_bias_bwd[l], a_log_fwd[l], a_log_bwd[l], d_skip[l],
                     ssm_norm_w[l], w_att_proj[l], w_ssm_proj[l], w_out[l], ln_g[l], ln_b[l])
        xp = _layer(xp, p, tables)
        xs = _layer(xs, p, tables)
    return (xp, xs)
```

```python
import numpy as np
import jax
import jax.numpy as jnp
from jax import lax
from jax.experimental import pallas as pl
from jax.experimental.pallas import tpu as pltpu

F32 = jnp.float32
BF16 = jnp.bfloat16

D_MODEL = 1024
GRID_W = 64
N_Q_HEADS = 16
N_KV_HEADS = 4
HEAD_DIM = 64
ATT_W = N_Q_HEADS * HEAD_DIM
KV_W = N_KV_HEADS * HEAD_DIM
AXIS_DIM = HEAD_DIM // 2
ROPE_THETA = 10000.0
D_INNER = 2 * D_MODEL
SSM_HEAD_DIM = 64
N_SSM_HEADS = D_INNER // SSM_HEAD_DIM
N_SSM_GROUPS = 4
D_STATE = 128
CONV_W = 5
CHUNK = 128
BC_W = N_SSM_GROUPS * D_STATE
CONV_CH = D_INNER + 2 * BC_W
RMS_EPS = 1e-6
LN_EPS = 1e-5
DEPTH = 1
ALPHA = (2 * DEPTH) ** 0.25
SPLITS = (ATT_W, KV_W, KV_W, ATT_W, D_INNER, CONV_CH, 2 * N_SSM_HEADS, 2 * D_MODEL)
SPLIT_POINTS = tuple(int(v) for v in np.cumsum(SPLITS)[:-1])

LANES = 128
HALO = 8
BF16_ROWS = 16
CONV_PHASES = 4
CONV_CHUNK = 512
SSD_BWD_CHUNKS_PER_STEP = 4
SSD_FWD_CHUNKS_PER_STEP = 2
GROUP_W = D_INNER // N_SSM_GROUPS
HEADS_PER_GROUP = N_SSM_HEADS // N_SSM_GROUPS
VMEM_LIMIT = 48 * 1024 * 1024
MAX_TOKEN_TILE = 512
MAX_QUERY_TILE = 512
ATTN_KEY_BLOCK = 256
Q_SCALE = float(np.log2(np.e)) * HEAD_DIM ** -0.5


def _cparams(sem):
    return pltpu.CompilerParams(dimension_semantics=sem, vmem_limit_bytes=VMEM_LIMIT)


def _dot(a, b):
    return jnp.dot(a, b, preferred_element_type=F32)


def _dot_nt(a, b):
    return lax.dot_general(a, b, (((1,), (1,)), ((), ())), preferred_element_type=F32)


def _dot_tn(a, b):
    return lax.dot_general(a, b, (((0,), (0,)), ((), ())), preferred_element_type=F32)


def _silu(x):
    return x / (1.0 + jnp.exp(-x))


def _split2(v):
    hi = v.astype(BF16)
    lo = (v - hi.astype(F32)).astype(BF16)
    return hi, lo


def _split3(v):
    hi = v.astype(BF16)
    r = v - hi.astype(F32)
    mid = r.astype(BF16)
    lo = (r - mid.astype(F32)).astype(BF16)
    return hi, mid, lo


def _qkv_kernel(x_ref, wt_ref, cos_ref, sin_ref, qw_ref, kw_ref, qt_ref, k_ref, vt_ref):
    xb = x_ref[0].astype(BF16)
    cosv = cos_ref[...]
    sinv = sin_ref[...]
    half = HEAD_DIM // 2

    def norm_rope(acc, w_ref, scale):
        x0, x1 = acc[:half], acc[half:]
        ms = (jnp.sum(x0 * x0, axis=0, keepdims=True) + jnp.sum(x1 * x1, axis=0, keepdims=True)) * (1.0 / HEAD_DIM)
        inv = lax.rsqrt(ms + RMS_EPS)
        a0 = x0 * inv * w_ref[:half]
        a1 = x1 * inv * w_ref[half:]
        return (a0 * cosv - a1 * sinv) * scale, (a0 * sinv + a1 * cosv) * scale

    rows = 4 * HEAD_DIM
    for c in range(ATT_W // rows):
        acc = _dot_nt(wt_ref[c * rows:(c + 1) * rows, :], xb)
        for h in range(rows // HEAD_DIM):
            o0, o1 = norm_rope(acc[h * HEAD_DIM:(h + 1) * HEAD_DIM], qw_ref, Q_SCALE)
            r0 = c * rows + h * HEAD_DIM
            qt_ref[0, r0:r0 + half, :] = o0.astype(BF16)
            qt_ref[0, r0 + half:r0 + HEAD_DIM, :] = o1.astype(BF16)
    acc = _dot_nt(wt_ref[ATT_W:ATT_W + KV_W, :], xb)
    for g in range(N_KV_HEADS):
        o0, o1 = norm_rope(acc[g * HEAD_DIM:(g + 1) * HEAD_DIM], kw_ref, 1.0)
        k_ref[0, g] = jnp.concatenate([o0, o1], axis=0).T.astype(BF16)
    vt_ref[0] = _dot_nt(wt_ref[ATT_W + KV_W:ATT_W + 2 * KV_W, :], xb).astype(BF16)


def _qkv_call(x, wt_qkv, cos_t, sin_t, qw, kw, tm):
    b, s, _ = x.shape
    half = HEAD_DIM // 2
    return pl.pallas_call(
        _qkv_kernel,
        grid=(b, s // tm),
        in_specs=[
            pl.BlockSpec((1, tm, D_MODEL), lambda i, j: (i, j, 0)),
            pl.BlockSpec(wt_qkv.shape, lambda i, j: (0, 0)),
            pl.BlockSpec((half, tm), lambda i, j: (0, j)),
            pl.BlockSpec((half, tm), lambda i, j: (0, j)),
            pl.BlockSpec(qw.shape, lambda i, j: (0, 0)),
            pl.BlockSpec(kw.shape, lambda i, j: (0, 0)),
        ],
        out_specs=[
            pl.BlockSpec((1, ATT_W, tm), lambda i, j: (i, 0, j)),
            pl.BlockSpec((1, N_KV_HEADS, tm, HEAD_DIM), lambda i, j: (i, 0, j, 0)),
            pl.BlockSpec((1, KV_W, tm), lambda i, j: (i, 0, j)),
        ],
        out_shape=[
            jax.ShapeDtypeStruct((b, ATT_W, s), BF16),
            jax.ShapeDtypeStruct((b, N_KV_HEADS, s, HEAD_DIM), BF16),
            jax.ShapeDtypeStruct((b, KV_W, s), BF16),
        ],
        compiler_params=_cparams(("parallel", "parallel")),
        name="qkv_proj",
    )(x, wt_qkv, cos_t, sin_t, qw, kw)


def _gz_kernel(x_ref, w_ref, g_ref, z_ref):
    xb = x_ref[0].astype(BF16)
    cw = 4 * LANES
    for c in range(ATT_W // cw):
        acc = _dot(xb, w_ref[:, c * cw:(c + 1) * cw])
        g_ref[0, :, c * cw:(c + 1) * cw] = _silu(acc).astype(BF16)
    for c in range(D_INNER // cw):
        acc = _dot(xb, w_ref[:, ATT_W + c * cw:ATT_W + (c + 1) * cw])
        z_ref[0, :, c * cw:(c + 1) * cw] = _silu(acc).astype(BF16)


def _gz_call(x, w_gz, tm):
    b, s, _ = x.shape
    return pl.pallas_call(
        _gz_kernel,
        grid=(b, s // tm),
        in_specs=[
            pl.BlockSpec((1, tm, D_MODEL), lambda i, j: (i, j, 0)),
            pl.BlockSpec(w_gz.shape, lambda i, j: (0, 0)),
        ],
        out_specs=[
            pl.BlockSpec((1, tm, ATT_W), lambda i, j: (i, j, 0)),
            pl.BlockSpec((1, tm, D_INNER), lambda i, j: (i, j, 0)),
        ],
        out_shape=[
            jax.ShapeDtypeStruct((b, s, ATT_W), BF16),
            jax.ShapeDtypeStruct((b, s, D_INNER), BF16),
        ],
        compiler_params=_cparams(("parallel", "parallel")),
        name="gate_proj",
    )(x, w_gz)


def _xbc_kernel(x_ref, xp_ref, xn_ref, w_ref, cw_ref, cb_ref, xs_ref, b_ref, c_ref,
                acc_a, acc_b, y_a, y_b):
    tm = x_ref.shape[1]
    accs = (acc_a, acc_b)
    ys = (y_a, y_b)
    j = pl.program_id(1)
    nj = pl.num_programs(1)
    prev = jnp.where(j > 0, xp_ref[0], 0.0)
    nxt = jnp.where(j < nj - 1, xn_ref[0], 0.0)
    xall = jnp.concatenate([prev, x_ref[0], nxt], axis=0).astype(BF16)
    cw = CONV_CHUNK
    slabs = cw // LANES
    outs = ([(xs_ref, c * cw) for c in range(D_INNER // cw)] + [(b_ref, c * cw) for c in range(BC_W // cw)]
            + [(c_ref, c * cw) for c in range(BC_W // cw)])

    def project(c):
        acc = _dot(xall, w_ref[:, c * cw:(c + 1) * cw])
        for s in range(slabs):
            accs[c % 2][s] = acc[:, s * LANES:(s + 1) * LANES]

    project(0)
    for c, (o_ref, off) in enumerate(outs):
        if c + 1 < len(outs):
            project(c + 1)
        for s in range(slabs):
            col = c * cw + s * LANES
            for p in range(CONV_PHASES):
                y = cb_ref[:, col:col + LANES]
                for k in range(CONV_W):
                    r0 = HALO - CONV_W // 2 + k + p
                    tap = accs[c % 2][s, pl.ds(r0, tm // CONV_PHASES, stride=CONV_PHASES), :]
                    y = y + tap * cw_ref[k:k + 1, col:col + LANES]
                ys[c % 2][s, pl.ds(p, tm // CONV_PHASES, stride=CONV_PHASES), :] = _silu(y)
        o_ref[0, :, off:off + cw] = jnp.concatenate([ys[c % 2][s] for s in range(slabs)], axis=1).astype(BF16)


def _xbc_call(x, w_xbc, conv_w, conv_b, tm):
    b, s, _ = x.shape
    hb = tm // HALO
    last = s // HALO - 1
    cw = CONV_CHUNK
    return pl.pallas_call(
        _xbc_kernel,
        grid=(b, s // tm),
        in_specs=[
            pl.BlockSpec((1, tm, D_MODEL), lambda i, j: (i, j, 0)),
            pl.BlockSpec((1, HALO, D_MODEL), lambda i, j: (i, jnp.maximum(j * hb - 1, 0), 0)),
            pl.BlockSpec((1, HALO, D_MODEL), lambda i, j: (i, jnp.minimum((j + 1) * hb, last), 0)),
            pl.BlockSpec(w_xbc.shape, lambda i, j: (0, 0)),
            pl.BlockSpec(conv_w.shape, lambda i, j: (0, 0)),
            pl.BlockSpec(conv_b.shape, lambda i, j: (0, 0)),
        ],
        out_specs=[
            pl.BlockSpec((1, tm, D_INNER), lambda i, j: (i, j, 0)),
            pl.BlockSpec((1, tm, BC_W), lambda i, j: (i, j, 0)),
            pl.BlockSpec((1, tm, BC_W), lambda i, j: (i, j, 0)),
        ],
        out_shape=[
            jax.ShapeDtypeStruct((b, s, D_INNER), BF16),
            jax.ShapeDtypeStruct((b, s, BC_W), BF16),
            jax.ShapeDtypeStruct((b, s, BC_W), BF16),
        ],
        scratch_shapes=[pltpu.VMEM((cw // LANES, tm + 2 * HALO, LANES), F32),
                        pltpu.VMEM((cw // LANES, tm + 2 * HALO, LANES), F32),
                        pltpu.VMEM((cw // LANES, tm, LANES), F32),
                        pltpu.VMEM((cw // LANES, tm, LANES), F32)],
        compiler_params=_cparams(("parallel", "parallel")),
        name="xbc_conv_proj",
    )(x, x, x, w_xbc, conv_w, conv_b)


def _gdt_kernel(x_ref, w_ref, bg_ref, bdt_ref, gate_ref, dt_ref):
    xb = x_ref[0].astype(BF16)
    cw = 4 * LANES
    for c in range(2 * D_MODEL // cw):
        acc = _dot(xb, w_ref[:, c * cw:(c + 1) * cw]) + bg_ref[:, c * cw:(c + 1) * cw]
        gate_ref[0, :, c * cw:(c + 1) * cw] = (1.0 / (1.0 + jnp.exp(-acc))).astype(BF16)
    v = _dot(xb, w_ref[:, 2 * D_MODEL:2 * D_MODEL + LANES]) + bdt_ref[...]
    dt_ref[0] = jnp.maximum(v, 0.0) + jnp.log1p(jnp.exp(-jnp.abs(v)))


def _gdt_call(x, w_gdt, b_gate, b_dt, tm):
    b, s, _ = x.shape
    return pl.pallas_call(
        _gdt_kernel,
        grid=(b, s // tm),
        in_specs=[
            pl.BlockSpec((1, tm, D_MODEL), lambda i, j: (i, j, 0)),
            pl.BlockSpec(w_gdt.shape, lambda i, j: (0, 0)),
            pl.BlockSpec(b_gate.shape, lambda i, j: (0, 0)),
            pl.BlockSpec(b_dt.shape, lambda i, j: (0, 0)),
        ],
        out_specs=[
            pl.BlockSpec((1, tm, 2 * D_MODEL), lambda i, j: (i, j, 0)),
            pl.BlockSpec((1, tm, LANES), lambda i, j: (i, j, 0)),
        ],
        out_shape=[
            jax.ShapeDtypeStruct((b, s, 2 * D_MODEL), BF16),
            jax.ShapeDtypeStruct((b, s, LANES), F32),
        ],
        compiler_params=_cparams(("parallel", "parallel")),
        name="gate_dt_proj",
    )(x, w_gdt, b_gate, b_dt)


def _attn_kernel(qt_ref, k_ref, vt_ref, g_ref, o_ref):
    k = k_ref[0, 0]
    vt = vt_ref[0]
    vt_aug = jnp.concatenate([vt, jnp.ones((BF16_ROWS, vt.shape[1]), BF16)], axis=0)
    rep = N_Q_HEADS // N_KV_HEADS
    tq = qt_ref.shape[2]
    s_len = k.shape[0]
    kb = min(ATTN_KEY_BLOCK, s_len)
    nblk = s_len // kb

    qt = jnp.concatenate([qt_ref[0, r * HEAD_DIM:(r + 1) * HEAD_DIM, :] for r in range(rep)], axis=1)
    m = acc = None
    st_next = _dot(k[:kb], qt)
    for j in range(nblk):
        st = st_next
        if j + 1 < nblk:
            st_next = _dot(k[(j + 1) * kb:(j + 2) * kb], qt)
        bm = jnp.max(st, axis=0, keepdims=True)
        m_new = bm if m is None else jnp.maximum(m, bm)
        pt = jnp.exp2(st - m_new).astype(BF16)
        part = _dot(vt_aug[:, j * kb:(j + 1) * kb], pt)
        acc = part if acc is None else acc * jnp.exp2(m - m_new) + part
        m = m_new
    on = acc[:HEAD_DIM] / acc[HEAD_DIM:HEAD_DIM + 1]
    o = jnp.concatenate([on[:, r * tq:(r + 1) * tq].T for r in range(rep)], axis=1)
    o_ref[0] = (o * g_ref[0].astype(F32)).astype(BF16)


def _attn_call(qt, k, vt, gsil, tq):
    b, _, s = qt.shape
    gw = ATT_W // N_KV_HEADS
    return pl.pallas_call(
        _attn_kernel,
        grid=(b, N_KV_HEADS, s // tq),
        in_specs=[
            pl.BlockSpec((1, gw, tq), lambda i, g, j: (i, g, j)),
            pl.BlockSpec((1, 1, s, HEAD_DIM), lambda i, g, j: (i, g, 0, 0)),
            pl.BlockSpec((1, HEAD_DIM, s), lambda i, g, j: (i, g, 0)),
            pl.BlockSpec((1, tq, gw), lambda i, g, j: (i, j, g)),
        ],
        out_specs=pl.BlockSpec((1, tq, gw), lambda i, g, j: (i, j, g)),
        out_shape=jax.ShapeDtypeStruct((b, s, ATT_W), BF16),
        compiler_params=_cparams(("parallel", "parallel", "parallel")),
        name="attention",
    )(qt, k, vt, gsil)


def _pack_hilo(v, lane):
    hi = v.astype(BF16).astype(F32)
    lo = (v - hi).astype(BF16).astype(F32)
    return jnp.where(lane < LANES // 2, hi, pltpu.roll(lo, LANES // 2, 1)).astype(BF16)


def _chunk_decay_terms(dt, alog_ref, tril_ref):
    a = -jnp.exp(alog_ref[...])
    da = dt * a
    tril = tril_ref[...]
    hi, mid, lo = _split3(da)
    pre = _dot(tril, hi) + _dot(tril, mid) + _dot(tril, lo)
    tot = pre[CHUNK - 1:CHUNK, :]
    return da, pre, tot


def _ssd_bwd_kernel(x_ref, b_ref, dt_ref, alog_ref, tril_ref, eb_ref, hb_ref, h_scr):
    c = pl.program_id(1)

    @pl.when(c == 0)
    def _():
        h_scr[...] = jnp.zeros_like(h_scr)

    cps = x_ref.shape[1] // CHUNK
    terms = []
    for u in range(cps):
        rows = slice(u * CHUNK, (u + 1) * CHUNK)
        dt = dt_ref[0, rows, :]
        da, pre, tot = _chunk_decay_terms(dt, alog_ref, tril_ref)
        wgt = jnp.exp(pre - da) * dt
        dec = jnp.exp(jnp.broadcast_to(tot, (BF16_ROWS, LANES)))
        packed = _pack_hilo(jnp.concatenate([wgt, dec], axis=0),
                            lax.broadcasted_iota(jnp.int32, (CHUNK + BF16_ROWS, LANES), 1))
        ex = _dot(packed, eb_ref[...])
        xw = (x_ref[0, rows, :].astype(F32) * ex[:CHUNK]).astype(BF16)
        upd = [_dot_tn(b_ref[0, rows, g * D_STATE:(g + 1) * D_STATE], xw[:, g * GROUP_W:(g + 1) * GROUP_W])
               for g in range(N_SSM_GROUPS)]
        terms.append((ex[CHUNK:CHUNK + 1], upd))
    for u in reversed(range(cps)):
        decx, upd = terms[u]
        hb_ref[0, u] = h_scr[...].astype(BF16)
        for g in range(N_SSM_GROUPS):
            cols = slice(g * GROUP_W, (g + 1) * GROUP_W)
            h_scr[:, cols] = h_scr[:, cols] * decx[:, cols] + upd[g]


def _ssd_bwd_call(xs, bm, dt, alog, tril, eb):
    b, s, _ = xs.shape
    cps = min(SSD_BWD_CHUNKS_PER_STEP, s // CHUNK)
    assert s % (cps * CHUNK) == 0
    nc = s // (cps * CHUNK)
    rev = lambda i, c: (i, nc - 1 - c, 0)
    return pl.pallas_call(
        _ssd_bwd_kernel,
        grid=(b, nc),
        in_specs=[
            pl.BlockSpec((1, cps * CHUNK, D_INNER), rev),
            pl.BlockSpec((1, cps * CHUNK, BC_W), rev),
            pl.BlockSpec((1, cps * CHUNK, LANES), rev),
            pl.BlockSpec(alog.shape, lambda i, c: (0, 0)),
            pl.BlockSpec(tril.shape, lambda i, c: (0, 0)),
            pl.BlockSpec(eb.shape, lambda i, c: (0, 0)),
        ],
        out_specs=pl.BlockSpec((1, cps, D_STATE, D_INNER), lambda i, c: (i, nc - 1 - c, 0, 0)),
        out_shape=jax.ShapeDtypeStruct((b, s // CHUNK, D_STATE, D_INNER), BF16),
        scratch_shapes=[pltpu.VMEM((D_STATE, D_INNER), F32)],
        compiler_params=_cparams(("parallel", "arbitrary")),
        name="ssd_bwd_states",
    )(xs, bm, dt, alog, tril, eb)


def _ssd_fwd_kernel(x_ref, b_ref, c_ref, dt_ref, z_ref, hb_ref, alog_ref, dskip_ref, nw_ref,
                    tril_ref, ef_ref, eb_ref, y_ref, h_scr):
    c = pl.program_id(1)

    @pl.when(c == 0)
    def _():
        h_scr[...] = jnp.zeros_like(h_scr)

    half = N_SSM_HEADS
    lane = lax.broadcasted_iota(jnp.int32, (CHUNK, LANES), 1)
    row = lax.broadcasted_iota(jnp.int32, (CHUNK, CHUNK), 0)
    col = lax.broadcasted_iota(jnp.int32, (CHUNK, CHUNK), 1)
    lower = col <= row
    eye = col == row
    first_head = lane < SSM_HEAD_DIM
    grp = (lane - half) >> 3
    ef = ef_ref[...]
    eb = eb_ref[...]

    def state_free_part(u):
        rows = slice(u * CHUNK, (u + 1) * CHUNK)
        dt = dt_ref[0, rows, :]
        da, pre, tot = _chunk_decay_terms(dt, alog_ref, tril_ref)
        acs = jnp.where(lane < half, pre, tot - pre + da)
        acsp_t = (acs - jnp.log(dt)).T
        xb = x_ref[0, rows, :]
        hb = hb_ref[0, u]
        y_diag = []
        y_off_b = []
        cb_diag = []
        for g in range(N_SSM_GROUPS):
            bg = b_ref[0, rows, g * D_STATE:(g + 1) * D_STATE]
            cg = c_ref[0, rows, g * D_STATE:(g + 1) * D_STATE]
            cb = _dot_nt(cg, bg)
            cb_diag.append(jnp.sum(jnp.where(eye, cb, 0.0), axis=-1, keepdims=True))
            for k in range(HEADS_PER_GROUP // 2):
                ms = []
                for h in (g * HEADS_PER_GROUP + 2 * k, g * HEADS_PER_GROUP + 2 * k + 1):
                    arg = jnp.where(lower,
                                    acs[:, h:h + 1] - acsp_t[h:h + 1, :],
                                    acs[:, half + h:half + h + 1] - acsp_t[half + h:half + h + 1, :])
                    ms.append((cb * jnp.exp(arg)).astype(BF16))
                pair = g * (HEADS_PER_GROUP // 2) + k
                xp = xb[:, pair * LANES:(pair + 1) * LANES]
                zero = jnp.zeros_like(xp)
                rhs = jnp.concatenate([jnp.where(first_head, xp, zero), jnp.where(first_head, zero, xp)], axis=0)
                y_diag.append(_dot(jnp.concatenate(ms, axis=1), rhs))
            y_off_b.append(_dot(cg, hb[:, g * GROUP_W:(g + 1) * GROUP_W]))
        y_diag = jnp.concatenate(y_diag, axis=1)
        y_off_b = jnp.concatenate(y_off_b, axis=1)

        diag_h = jnp.zeros((CHUNK, LANES), F32)
        for g in range(N_SSM_GROUPS):
            diag_h = jnp.where(grp == g, cb_diag[g], diag_h)
        coef = dskip_ref[...] + diag_h * dt
        scale = jnp.exp(acs)
        wgt = jnp.exp(tot - pre) * dt
        dec = jnp.exp(jnp.broadcast_to(tot, (BF16_ROWS, LANES)))
        pk_scale = _pack_hilo(scale, lane)
        scale_f = _dot(pk_scale, ef)
        scale_b = _dot(pk_scale, eb)
        coef_x = _dot(_pack_hilo(coef, lane), eb)
        pk_wd = _pack_hilo(jnp.concatenate([wgt, dec], axis=0),
                           lax.broadcasted_iota(jnp.int32, (CHUNK + BF16_ROWS, LANES), 1))
        wd_x = _dot(pk_wd, ef)
        xf = xb.astype(F32)
        y_part = y_diag + y_off_b * scale_b + coef_x * xf
        xw = (xf * wd_x[:CHUNK]).astype(BF16)
        upd = [_dot_tn(b_ref[0, rows, g * D_STATE:(g + 1) * D_STATE], xw[:, g * GROUP_W:(g + 1) * GROUP_W])
               for g in range(N_SSM_GROUPS)]
        return y_part, scale_f, wd_x[CHUNK:CHUNK + 1], upd

    cps = x_ref.shape[1] // CHUNK
    parts = [state_free_part(u) for u in range(cps)]
    for u in range(cps):
        rows = slice(u * CHUNK, (u + 1) * CHUNK)
        y_part, scale_f, decx, upd = parts[u]
        for g in range(N_SSM_GROUPS):
            cols = slice(g * GROUP_W, (g + 1) * GROUP_W)
            cg = c_ref[0, rows, g * D_STATE:(g + 1) * D_STATE]
            y_off_f = _dot(cg, h_scr[:, cols].astype(BF16))
            yg = (y_part[:, cols] + y_off_f * scale_f[:, cols]) * z_ref[0, rows, cols].astype(F32)
            ms = jnp.mean(yg * yg, axis=-1, keepdims=True)
            y_ref[0, rows, cols] = (yg * lax.rsqrt(ms + RMS_EPS) * nw_ref[:, cols]).astype(BF16)
            h_scr[:, cols] = h_scr[:, cols] * decx[:, cols] + upd[g]


def _ssd_fwd_call(xs, bm, cm, dt, zsil, hb, alog, dskip, nw, tril, ef, eb):
    b, s, _ = xs.shape
    cps = min(SSD_FWD_CHUNKS_PER_STEP, s // CHUNK)
    assert s % (cps * CHUNK) == 0
    nc = s // (cps * CHUNK)
    fwd = lambda i, c: (i, c, 0)
    const = lambda i, c: (0, 0)
    return pl.pallas_call(
        _ssd_fwd_kernel,
        grid=(b, nc),
        in_specs=[
            pl.BlockSpec((1, cps * CHUNK, D_INNER), fwd),
            pl.BlockSpec((1, cps * CHUNK, BC_W), fwd),
            pl.BlockSpec((1, cps * CHUNK, BC_W), fwd),
            pl.BlockSpec((1, cps * CHUNK, LANES), fwd),
            pl.BlockSpec((1, cps * CHUNK, D_INNER), fwd),
            pl.BlockSpec((1, cps, D_STATE, D_INNER), lambda i, c: (i, c, 0, 0)),
            pl.BlockSpec(alog.shape, const),
            pl.BlockSpec(dskip.shape, const),
            pl.BlockSpec(nw.shape, const),
            pl.BlockSpec(tril.shape, const),
            pl.BlockSpec(ef.shape, const),
            pl.BlockSpec(eb.shape, const),
        ],
        out_specs=pl.BlockSpec((1, cps * CHUNK, D_INNER), fwd),
        out_shape=jax.ShapeDtypeStruct((b, s, D_INNER), BF16),
        scratch_shapes=[pltpu.VMEM((D_STATE, D_INNER), F32)],
        compiler_params=_cparams(("parallel", "arbitrary")),
        name="ssd_fwd",
    )(xs, bm, cm, dt, zsil, hb, alog, dskip, nw, tril, ef, eb)


def _out_kernel(x_ref, att_ref, y_ref, gate_ref, wa_ref, ws_ref, wo_ref, lg_ref, lb_ref, o_ref):
    att = _dot(att_ref[0], wa_ref[...])
    ssm = _dot(y_ref[0], ws_ref[...])
    ga = gate_ref[0, :, :D_MODEL].astype(F32)
    gs = gate_ref[0, :, D_MODEL:].astype(F32)
    mixed = ga * att + gs * ssm
    out = _dot(mixed.astype(BF16), wo_ref[...])
    r = ALPHA * x_ref[0] + out
    mu = jnp.mean(r, axis=-1, keepdims=True)
    d = r - mu
    var = jnp.mean(d * d, axis=-1, keepdims=True)
    o_ref[0] = d * lax.rsqrt(var + LN_EPS) * lg_ref[...] + lb_ref[...]


def _out_call(x, att, y, gates, wa, ws, wo, lg, lb, tm):
    b, s, _ = x.shape
    tok = lambda i, j: (i, j, 0)
    const = lambda i, j: (0, 0)
    return pl.pallas_call(
        _out_kernel,
        grid=(b, s // tm),
        in_specs=[
            pl.BlockSpec((1, tm, D_MODEL), tok),
            pl.BlockSpec((1, tm, ATT_W), tok),
            pl.BlockSpec((1, tm, D_INNER), tok),
            pl.BlockSpec((1, tm, 2 * D_MODEL), tok),
            pl.BlockSpec(wa.shape, const),
            pl.BlockSpec(ws.shape, const),
            pl.BlockSpec(wo.shape, const),
            pl.BlockSpec(lg.shape, const),
            pl.BlockSpec(lb.shape, const),
        ],
        out_specs=pl.BlockSpec((1, tm, D_MODEL), tok),
        out_shape=jax.ShapeDtypeStruct((b, s, D_MODEL), F32),
        compiler_params=_cparams(("parallel", "parallel")),
        name="merge_out_ln",
    )(x, att, y, gates, wa, ws, wo, lg, lb)


def _rope_tables(seq_len):
    rows = seq_len // GRID_W
    row = jnp.repeat(jnp.arange(rows, dtype=F32), GRID_W)
    col = jnp.tile(jnp.arange(GRID_W, dtype=F32), rows)
    freqs = ROPE_THETA ** (-jnp.arange(0, AXIS_DIM, 2, dtype=F32) / AXIS_DIM)
    ang = jnp.concatenate([row[:, None] * freqs, col[:, None] * freqs], axis=-1)
    return jnp.cos(ang).T, jnp.sin(ang).T


def _expand_matrix(lane_offset):
    r = np.arange(LANES)[:, None] % (LANES // 2)
    head = np.arange(D_INNER)[None, :] // SSM_HEAD_DIM
    return jnp.asarray((r == head + lane_offset).astype(np.float32), dtype=BF16)


def _layer(x, p, tables):
    b, s, _ = x.shape
    tm = min(MAX_TOKEN_TILE, s)
    tq = min(MAX_QUERY_TILE, s)
    cos_t, sin_t = tables
    qt, k, vt = _qkv_call(x, p["wt_qkv"], cos_t[:, :s], sin_t[:, :s], p["qw"][:, :tm], p["kw"][:, :tm], tm)
    gsil, zsil = _gz_call(x, p["w_gz"], tm)
    xs, bm, cm = _xbc_call(x, p["w_xbc"], p["conv_w"], p["conv_b"], tm)
    gates, dt = _gdt_call(x, p["w_gdt"], p["b_gate"], p["b_dt"], tm)
    att = _attn_call(qt, k, vt, gsil, tq)
    hb = _ssd_bwd_call(xs, bm, dt, p["alog"], p["tril"], p["eb"])
    y = _ssd_fwd_call(xs, bm, cm, dt, zsil, hb, p["alog"], p["dskip"], p["nw"], p["tril"], p["ef"], p["eb"])
    return _out_call(x, att, y, gates, p["wa"], p["ws"], p["wo"], p["lg"], p["lb"], tm)


def _prepare(w_in, b_gate, q_norm_w, k_norm_w, conv_w, conv_b, dt_bias_fwd, dt_bias_bwd,
             a_log_fwd, a_log_bwd, d_skip, ssm_norm_w, w_att_proj, w_ssm_proj, w_out, ln_g, ln_b):
    wq, wk, wv, wg, wz, wxbc, wdt, wgate = jnp.split(w_in, SPLIT_POINTS, axis=-1)
    perm = np.concatenate([np.arange(0, HEAD_DIM, 2), np.arange(1, HEAD_DIM, 2)])
    qperm = (np.arange(N_Q_HEADS)[:, None] * HEAD_DIM + perm[None, :]).reshape(-1)
    kperm = (np.arange(N_KV_HEADS)[:, None] * HEAD_DIM + perm[None, :]).reshape(-1)
    pad = LANES - 2 * N_SSM_HEADS
    zeros_h = jnp.zeros((N_SSM_HEADS,), F32)
    zeros_p = jnp.zeros((pad,), F32)
    return {
        "wt_qkv": jnp.concatenate([wq[:, qperm], wk[:, kperm], wv], axis=1).T.astype(BF16),
        "qw": jnp.broadcast_to(q_norm_w[perm][:, None], (HEAD_DIM, MAX_TOKEN_TILE)),
        "kw": jnp.broadcast_to(k_norm_w[perm][:, None], (HEAD_DIM, MAX_TOKEN_TILE)),
        "w_gz": jnp.concatenate([wg, wz], axis=1).astype(BF16),
        "w_xbc": wxbc.astype(BF16),
        "conv_w": conv_w,
        "conv_b": conv_b[None, :],
        "w_gdt": jnp.concatenate([wgate, wdt, jnp.zeros((D_MODEL, pad), F32)], axis=1).astype(BF16),
        "b_gate": b_gate[None, :],
        "b_dt": jnp.concatenate([dt_bias_fwd, dt_bias_bwd, zeros_p])[None, :],
        "alog": jnp.concatenate([a_log_fwd, a_log_bwd, zeros_p])[None, :],
        "dskip": jnp.concatenate([zeros_h, d_skip, zeros_p])[None, :],
        "nw": ssm_norm_w[None, :],
        "tril": jnp.asarray(np.tril(np.ones((CHUNK, CHUNK), np.float32)), dtype=BF16),
        "ef": _expand_matrix(0),
        "eb": _expand_matrix(N_SSM_HEADS),
        "wa": w_att_proj.astype(BF16),
        "ws": w_ssm_proj.astype(BF16),
        "wo": w_out.astype(BF16),
        "lg": ln_g[None, :],
        "lb": ln_b[None, :],
    }


def kernel(x_prompt, x_sample, w_in, b_gate, q_norm_w, k_norm_w, conv_w, conv_b, dt_bias_fwd,
           dt_bias_bwd, a_log_fwd, a_log_bwd, d_skip, ssm_norm_w, w_att_proj, w_ssm_proj, w_out,
           ln_g, ln_b):
    xp, xs = x_prompt, x_sample
    tables = _rope_tables(max(xp.shape[1], xs.shape[1]))
    for l in range(w_in.shape[0]):
        p = _prepare(w_in[l], b_gate[l], q_norm_w[l], k_norm_w[l], conv_w[l], conv_b[l],
                     dt_bias_fwd[l], dt_bias_bwd[l], a_log_fwd[l], a_log_bwd[l], d_skip[l],
                     ssm_norm_w[l], w_att_proj[l], w_ssm_proj[l], w_out[l], ln_g[l], ln_b[l])
        xp = _layer(xp, p, tables)
        xs = _layer(xs, p, tables)
    return (xp, xs)
```

```python
import numpy as np
import jax
import jax.numpy as jnp
from jax import lax
from jax.experimental import pallas as pl
from jax.experimental.pallas import tpu as pltpu

F32 = jnp.float32
BF16 = jnp.bfloat16

D_MODEL = 1024
GRID_W = 64
N_Q_HEADS = 16
N_KV_HEADS = 4
HEAD_DIM = 64
ATT_W = N_Q_HEADS * HEAD_DIM
KV_W = N_KV_HEADS * HEAD_DIM
AXIS_DIM = HEAD_DIM // 2
ROPE_THETA = 10000.0
D_INNER = 2 * D_MODEL
SSM_HEAD_DIM = 64
N_SSM_HEADS = D_INNER // SSM_HEAD_DIM
N_SSM_GROUPS = 4
D_STATE = 128
CONV_W = 5
CHUNK = 128
BC_W = N_SSM_GROUPS * D_STATE
CONV_CH = D_INNER + 2 * BC_W
RMS_EPS = 1e-6
LN_EPS = 1e-5
DEPTH = 1
ALPHA = (2 * DEPTH) ** 0.25
SPLITS = (ATT_W, KV_W, KV_W, ATT_W, D_INNER, CONV_CH, 2 * N_SSM_HEADS, 2 * D_MODEL)
SPLIT_POINTS = tuple(int(v) for v in np.cumsum(SPLITS)[:-1])

LANES = 128
HALO = 8
BF16_ROWS = 16
CONV_PHASES = 4
CONV_CHUNK = 512
SSD_BWD_CHUNKS_PER_STEP = 8
SSD_FWD_CHUNKS_PER_STEP = 4
GROUP_W = D_INNER // N_SSM_GROUPS
HEADS_PER_GROUP = N_SSM_HEADS // N_SSM_GROUPS
VMEM_LIMIT = 48 * 1024 * 1024
MAX_TOKEN_TILE = 512
MAX_QUERY_TILE = 1024
ATTN_KEY_BLOCK = 256
Q_SCALE = float(np.log2(np.e)) * HEAD_DIM ** -0.5


def _cparams(sem):
    return pltpu.CompilerParams(dimension_semantics=sem, vmem_limit_bytes=VMEM_LIMIT)


def _dot(a, b):
    return jnp.dot(a, b, preferred_element_type=F32)


def _dot_nt(a, b):
    return lax.dot_general(a, b, (((1,), (1,)), ((), ())), preferred_element_type=F32)


def _dot_tn(a, b):
    return lax.dot_general(a, b, (((0,), (0,)), ((), ())), preferred_element_type=F32)


def _silu(x):
    return x / (1.0 + jnp.exp(-x))


def _split2(v):
    hi = v.astype(BF16)
    lo = (v - hi.astype(F32)).astype(BF16)
    return hi, lo


def _split3(v):
    hi = v.astype(BF16)
    r = v - hi.astype(F32)
    mid = r.astype(BF16)
    lo = (r - mid.astype(F32)).astype(BF16)
    return hi, mid, lo


def _qkv_kernel(x_ref, wt_ref, cos_ref, sin_ref, qw_ref, kw_ref, qt_ref, k_ref, vt_ref):
    xb = x_ref[0].astype(BF16)
    cosv = cos_ref[...]
    sinv = sin_ref[...]
    half = HEAD_DIM // 2

    def norm_rope(acc, w_ref, scale):
        x0, x1 = acc[:half], acc[half:]
        ms = (jnp.sum(x0 * x0, axis=0, keepdims=True) + jnp.sum(x1 * x1, axis=0, keepdims=True)) * (1.0 / HEAD_DIM)
        inv = lax.rsqrt(ms + RMS_EPS)
        a0 = x0 * inv * w_ref[:half]
        a1 = x1 * inv * w_ref[half:]
        return (a0 * cosv - a1 * sinv) * scale, (a0 * sinv + a1 * cosv) * scale

    rows = 4 * HEAD_DIM
    for c in range(ATT_W // rows):
        acc = _dot_nt(wt_ref[c * rows:(c + 1) * rows, :], xb)
        for h in range(rows // HEAD_DIM):
            o0, o1 = norm_rope(acc[h * HEAD_DIM:(h + 1) * HEAD_DIM], qw_ref, Q_SCALE)
            r0 = c * rows + h * HEAD_DIM
            qt_ref[0, r0:r0 + half, :] = o0.astype(BF16)
            qt_ref[0, r0 + half:r0 + HEAD_DIM, :] = o1.astype(BF16)
    acc = _dot_nt(wt_ref[ATT_W:ATT_W + KV_W, :], xb)
    for g in range(N_KV_HEADS):
        o0, o1 = norm_rope(acc[g * HEAD_DIM:(g + 1) * HEAD_DIM], kw_ref, 1.0)
        k_ref[0, g] = jnp.concatenate([o0, o1], axis=0).T.astype(BF16)
    vt_ref[0] = _dot_nt(wt_ref[ATT_W + KV_W:ATT_W + 2 * KV_W, :], xb).astype(BF16)


def _qkv_call(x, wt_qkv, cos_t, sin_t, qw, kw, tm):
    b, s, _ = x.shape
    half = HEAD_DIM // 2
    return pl.pallas_call(
        _qkv_kernel,
        grid=(b, s // tm),
        in_specs=[
            pl.BlockSpec((1, tm, D_MODEL), lambda i, j: (i, j, 0)),
            pl.BlockSpec(wt_qkv.shape, lambda i, j: (0, 0)),
            pl.BlockSpec((half, tm), lambda i, j: (0, j)),
            pl.BlockSpec((half, tm), lambda i, j: (0, j)),
            pl.BlockSpec(qw.shape, lambda i, j: (0, 0)),
            pl.BlockSpec(kw.shape, lambda i, j: (0, 0)),
        ],
        out_specs=[
            pl.BlockSpec((1, ATT_W, tm), lambda i, j: (i, 0, j)),
            pl.BlockSpec((1, N_KV_HEADS, tm, HEAD_DIM), lambda i, j: (i, 0, j, 0)),
            pl.BlockSpec((1, KV_W, tm), lambda i, j: (i, 0, j)),
        ],
        out_shape=[
            jax.ShapeDtypeStruct((b, ATT_W, s), BF16),
            jax.ShapeDtypeStruct((b, N_KV_HEADS, s, HEAD_DIM), BF16),
            jax.ShapeDtypeStruct((b, KV_W, s), BF16),
        ],
        compiler_params=_cparams(("parallel", "parallel")),
        name="qkv_proj",
    )(x, wt_qkv, cos_t, sin_t, qw, kw)


def _gates_kernel(x_ref, w_ref, bg_ref, bdt_ref, g_ref, z_ref, gate_ref, dt_ref):
    xb = x_ref[0].astype(BF16)
    cw = 4 * LANES
    off = 0
    for c in range(ATT_W // cw):
        acc = _dot(xb, w_ref[:, off + c * cw:off + (c + 1) * cw])
        g_ref[0, :, c * cw:(c + 1) * cw] = _silu(acc).astype(BF16)
    off += ATT_W
    for c in range(D_INNER // cw):
        acc = _dot(xb, w_ref[:, off + c * cw:off + (c + 1) * cw])
        z_ref[0, :, c * cw:(c + 1) * cw] = _silu(acc).astype(BF16)
    off += D_INNER
    for c in range(2 * D_MODEL // cw):
        acc = _dot(xb, w_ref[:, off + c * cw:off + (c + 1) * cw]) + bg_ref[:, c * cw:(c + 1) * cw]
        gate_ref[0, :, c * cw:(c + 1) * cw] = (1.0 / (1.0 + jnp.exp(-acc))).astype(BF16)
    off += 2 * D_MODEL
    v = _dot(xb, w_ref[:, off:off + LANES]) + bdt_ref[...]
    dt_ref[0] = jnp.maximum(v, 0.0) + jnp.log1p(jnp.exp(-jnp.abs(v)))


def _gates_call(x, w_gates, b_gate, b_dt, tm):
    b, s, _ = x.shape
    tok = lambda i, j: (i, j, 0)
    const = lambda i, j: (0, 0)
    return pl.pallas_call(
        _gates_kernel,
        grid=(b, s // tm),
        in_specs=[
            pl.BlockSpec((1, tm, D_MODEL), tok),
            pl.BlockSpec(w_gates.shape, const),
            pl.BlockSpec(b_gate.shape, const),
            pl.BlockSpec(b_dt.shape, const),
        ],
        out_specs=[
            pl.BlockSpec((1, tm, ATT_W), tok),
            pl.BlockSpec((1, tm, D_INNER), tok),
            pl.BlockSpec((1, tm, 2 * D_MODEL), tok),
            pl.BlockSpec((1, tm, LANES), tok),
        ],
        out_shape=[
            jax.ShapeDtypeStruct((b, s, ATT_W), BF16),
            jax.ShapeDtypeStruct((b, s, D_INNER), BF16),
            jax.ShapeDtypeStruct((b, s, 2 * D_MODEL), BF16),
            jax.ShapeDtypeStruct((b, s, LANES), F32),
        ],
        compiler_params=_cparams(("parallel", "parallel")),
        name="gates_proj",
    )(x, w_gates, b_gate, b_dt)


def _xbc_kernel(x_ref, xp_ref, xn_ref, w_ref, cw_ref, cb_ref, xs_ref, b_ref, c_ref,
                acc_a, acc_b, y_a, y_b):
    tm = x_ref.shape[1]
    accs = (acc_a, acc_b)
    ys = (y_a, y_b)
    j = pl.program_id(1)
    nj = pl.num_programs(1)
    prev = jnp.where(j > 0, xp_ref[0], 0.0)
    nxt = jnp.where(j < nj - 1, xn_ref[0], 0.0)
    xall = jnp.concatenate([prev, x_ref[0], nxt], axis=0).astype(BF16)
    cw = CONV_CHUNK
    slabs = cw // LANES
    outs = ([(xs_ref, c * cw) for c in range(D_INNER // cw)] + [(b_ref, c * cw) for c in range(BC_W // cw)]
            + [(c_ref, c * cw) for c in range(BC_W // cw)])

    def project(c):
        acc = _dot(xall, w_ref[:, c * cw:(c + 1) * cw])
        for s in range(slabs):
            accs[c % 2][s] = acc[:, s * LANES:(s + 1) * LANES]

    project(0)
    for c, (o_ref, off) in enumerate(outs):
        if c + 1 < len(outs):
            project(c + 1)
        for s in range(slabs):
            col = c * cw + s * LANES
            for p in range(CONV_PHASES):
                y = cb_ref[:, col:col + LANES]
                for k in range(CONV_W):
                    r0 = HALO - CONV_W // 2 + k + p
                    tap = accs[c % 2][s, pl.ds(r0, tm // CONV_PHASES, stride=CONV_PHASES), :]
                    y = y + tap * cw_ref[k:k + 1, col:col + LANES]
                ys[c % 2][s, pl.ds(p, tm // CONV_PHASES, stride=CONV_PHASES), :] = _silu(y)
        o_ref[0, :, off:off + cw] = jnp.concatenate([ys[c % 2][s] for s in range(slabs)], axis=1).astype(BF16)


def _xbc_call(x, w_xbc, conv_w, conv_b, tm):
    b, s, _ = x.shape
    hb = tm // HALO
    last = s // HALO - 1
    cw = CONV_CHUNK
    return pl.pallas_call(
        _xbc_kernel,
        grid=(b, s // tm),
        in_specs=[
            pl.BlockSpec((1, tm, D_MODEL), lambda i, j: (i, j, 0)),
            pl.BlockSpec((1, HALO, D_MODEL), lambda i, j: (i, jnp.maximum(j * hb - 1, 0), 0)),
            pl.BlockSpec((1, HALO, D_MODEL), lambda i, j: (i, jnp.minimum((j + 1) * hb, last), 0)),
            pl.BlockSpec(w_xbc.shape, lambda i, j: (0, 0)),
            pl.BlockSpec(conv_w.shape, lambda i, j: (0, 0)),
            pl.BlockSpec(conv_b.shape, lambda i, j: (0, 0)),
        ],
        out_specs=[
            pl.BlockSpec((1, tm, D_INNER), lambda i, j: (i, j, 0)),
            pl.BlockSpec((1, tm, BC_W), lambda i, j: (i, j, 0)),
            pl.BlockSpec((1, tm, BC_W), lambda i, j: (i, j, 0)),
        ],
        out_shape=[
            jax.ShapeDtypeStruct((b, s, D_INNER), BF16),
            jax.ShapeDtypeStruct((b, s, BC_W), BF16),
            jax.ShapeDtypeStruct((b, s, BC_W), BF16),
        ],
        scratch_shapes=[pltpu.VMEM((cw // LANES, tm + 2 * HALO, LANES), F32),
                        pltpu.VMEM((cw // LANES, tm + 2 * HALO, LANES), F32),
                        pltpu.VMEM((cw // LANES, tm, LANES), F32),
                        pltpu.VMEM((cw // LANES, tm, LANES), F32)],
        compiler_params=_cparams(("parallel", "parallel")),
        name="xbc_conv_proj",
    )(x, x, x, w_xbc, conv_w, conv_b)


def _attn_kernel(qt_ref, k_ref, vt_ref, g_ref, o_ref):
    k = k_ref[0, 0]
    vt = vt_ref[0]
    vt_aug = jnp.concatenate([vt, jnp.ones((BF16_ROWS, vt.shape[1]), BF16)], axis=0)
    rep = N_Q_HEADS // N_KV_HEADS
    tq = qt_ref.shape[2]
    s_len = k.shape[0]
    kb = min(ATTN_KEY_BLOCK, s_len)
    nblk = s_len // kb

    qt = jnp.concatenate([qt_ref[0, r * HEAD_DIM:(r + 1) * HEAD_DIM, :] for r in range(rep)], axis=1)
    m = acc = None
    st_next = _dot(k[:kb], qt)
    for j in range(nblk):
        st = st_next
        if j + 1 < nblk:
            st_next = _dot(k[(j + 1) * kb:(j + 2) * kb], qt)
        bm = jnp.max(st, axis=0, keepdims=True)
        m_new = bm if m is None else jnp.maximum(m, bm)
        pt = jnp.exp2(st - m_new).astype(BF16)
        part = _dot(vt_aug[:, j * kb:(j + 1) * kb], pt)
        acc = part if acc is None else acc * jnp.exp2(m - m_new) + part
        m = m_new
    on = acc[:HEAD_DIM] / acc[HEAD_DIM:HEAD_DIM + 1]
    o = jnp.concatenate([on[:, r * tq:(r + 1) * tq].T for r in range(rep)], axis=1)
    o_ref[0] = (o * g_ref[0].astype(F32)).astype(BF16)


def _attn_call(qt, k, vt, gsil, tq):
    b, _, s = qt.shape
    gw = ATT_W // N_KV_HEADS
    return pl.pallas_call(
        _attn_kernel,
        grid=(b, N_KV_HEADS, s // tq),
        in_specs=[
            pl.BlockSpec((1, gw, tq), lambda i, g, j: (i, g, j)),
            pl.BlockSpec((1, 1, s, HEAD_DIM), lambda i, g, j: (i, g, 0, 0)),
            pl.BlockSpec((1, HEAD_DIM, s), lambda i, g, j: (i, g, 0)),
            pl.BlockSpec((1, tq, gw), lambda i, g, j: (i, j, g)),
        ],
        out_specs=pl.BlockSpec((1, tq, gw), lambda i, g, j: (i, j, g)),
        out_shape=jax.ShapeDtypeStruct((b, s, ATT_W), BF16),
        compiler_params=_cparams(("parallel", "parallel", "parallel")),
        name="attention",
    )(qt, k, vt, gsil)


def _pack_hilo(v, lane):
    hi = v.astype(BF16).astype(F32)
    lo = (v - hi).astype(BF16).astype(F32)
    return jnp.where(lane < LANES // 2, hi, pltpu.roll(lo, LANES // 2, 1)).astype(BF16)


def _chunk_decay_terms(dt, alog_ref, tril_ref):
    a = -jnp.exp(alog_ref[...])
    da = dt * a
    tril = tril_ref[...]
    hi, mid, lo = _split3(da)
    pre = _dot(tril, hi) + _dot(tril, mid) + _dot(tril, lo)
    tot = pre[CHUNK - 1:CHUNK, :]
    return da, pre, tot


def _ssd_bwd_kernel(x_ref, b_ref, dt_ref, alog_ref, tril_ref, eb_ref, hb_ref, h_scr):
    c = pl.program_id(1)

    @pl.when(c == 0)
    def _():
        h_scr[...] = jnp.zeros_like(h_scr)

    cps = x_ref.shape[1] // CHUNK
    terms = []
    for u in range(cps):
        rows = slice(u * CHUNK, (u + 1) * CHUNK)
        dt = dt_ref[0, rows, :]
        da, pre, tot = _chunk_decay_terms(dt, alog_ref, tril_ref)
        wgt = jnp.exp(pre - da) * dt
        dec = jnp.exp(jnp.broadcast_to(tot, (BF16_ROWS, LANES)))
        packed = _pack_hilo(jnp.concatenate([wgt, dec], axis=0),
                            lax.broadcasted_iota(jnp.int32, (CHUNK + BF16_ROWS, LANES), 1))
        ex = _dot(packed, eb_ref[...])
        xw = (x_ref[0, rows, :].astype(F32) * ex[:CHUNK]).astype(BF16)
        upd = [_dot_tn(b_ref[0, rows, g * D_STATE:(g + 1) * D_STATE], xw[:, g * GROUP_W:(g + 1) * GROUP_W])
               for g in range(N_SSM_GROUPS)]
        terms.append((ex[CHUNK:CHUNK + 1], upd))
    for u in reversed(range(cps)):
        decx, upd = terms[u]
        hb_ref[0, u] = h_scr[...].astype(BF16)
        for g in range(N_SSM_GROUPS):
            cols = slice(g * GROUP_W, (g + 1) * GROUP_W)
            h_scr[:, cols] = h_scr[:, cols] * decx[:, cols] + upd[g]


def _ssd_bwd_call(xs, bm, dt, alog, tril, eb):
    b, s, _ = xs.shape
    cps = min(SSD_BWD_CHUNKS_PER_STEP, s // CHUNK)
    assert s % (cps * CHUNK) == 0
    nc = s // (cps * CHUNK)
    rev = lambda i, c: (i, nc - 1 - c, 0)
    return pl.pallas_call(
        _ssd_bwd_kernel,
        grid=(b, nc),
        in_specs=[
            pl.BlockSpec((1, cps * CHUNK, D_INNER), rev),
            pl.BlockSpec((1, cps * CHUNK, BC_W), rev),
            pl.BlockSpec((1, cps * CHUNK, LANES), rev),
            pl.BlockSpec(alog.shape, lambda i, c: (0, 0)),
            pl.BlockSpec(tril.shape, lambda i, c: (0, 0)),
            pl.BlockSpec(eb.shape, lambda i, c: (0, 0)),
        ],
        out_specs=pl.BlockSpec((1, cps, D_STATE, D_INNER), lambda i, c: (i, nc - 1 - c, 0, 0)),
        out_shape=jax.ShapeDtypeStruct((b, s // CHUNK, D_STATE, D_INNER), BF16),
        scratch_shapes=[pltpu.VMEM((D_STATE, D_INNER), F32)],
        compiler_params=_cparams(("parallel", "arbitrary")),
        name="ssd_bwd_states",
    )(xs, bm, dt, alog, tril, eb)


def _ssd_fwd_kernel(x_ref, b_ref, c_ref, dt_ref, z_ref, hb_ref, alog_ref, dskip_ref, nw_ref,
                    tril_ref, ef_ref, eb_ref, y_ref, h_scr):
    c = pl.program_id(1)

    @pl.when(c == 0)
    def _():
        h_scr[...] = jnp.zeros_like(h_scr)

    half = N_SSM_HEADS
    lane = lax.broadcasted_iota(jnp.int32, (CHUNK, LANES), 1)
    row = lax.broadcasted_iota(jnp.int32, (CHUNK, CHUNK), 0)
    col = lax.broadcasted_iota(jnp.int32, (CHUNK, CHUNK), 1)
    lower = col <= row
    eye = col == row
    first_head = lane < SSM_HEAD_DIM
    grp = (lane - half) >> 3
    ef = ef_ref[...]
    eb = eb_ref[...]

    def state_free_part(u):
        rows = slice(u * CHUNK, (u + 1) * CHUNK)
        dt = dt_ref[0, rows, :]
        da, pre, tot = _chunk_decay_terms(dt, alog_ref, tril_ref)
        acs = jnp.where(lane < half, pre, tot - pre + da)
        acsp_t = (acs - jnp.log(dt)).T
        xb = x_ref[0, rows, :]
        hb = hb_ref[0, u]
        y_diag = []
        y_off_b = []
        cb_diag = []
        for g in range(N_SSM_GROUPS):
            bg = b_ref[0, rows, g * D_STATE:(g + 1) * D_STATE]
            cg = c_ref[0, rows, g * D_STATE:(g + 1) * D_STATE]
            cb = _dot_nt(cg, bg)
            cb_diag.append(jnp.sum(jnp.where(eye, cb, 0.0), axis=-1, keepdims=True))
            for k in range(HEADS_PER_GROUP // 2):
                ms = []
                for h in (g * HEADS_PER_GROUP + 2 * k, g * HEADS_PER_GROUP + 2 * k + 1):
                    arg = jnp.where(lower,
                                    acs[:, h:h + 1] - acsp_t[h:h + 1, :],
                                    acs[:, half + h:half + h + 1] - acsp_t[half + h:half + h + 1, :])
                    ms.append((cb * jnp.exp(arg)).astype(BF16))
                pair = g * (HEADS_PER_GROUP // 2) + k
                xp = xb[:, pair * LANES:(pair + 1) * LANES]
                zero = jnp.zeros_like(xp)
                rhs = jnp.concatenate([jnp.where(first_head, xp, zero), jnp.where(first_head, zero, xp)], axis=0)
                y_diag.append(_dot(jnp.concatenate(ms, axis=1), rhs))
            y_off_b.append(_dot(cg, hb[:, g * GROUP_W:(g + 1) * GROUP_W]))
        y_diag = jnp.concatenate(y_diag, axis=1)
        y_off_b = jnp.concatenate(y_off_b, axis=1)

        diag_h = jnp.zeros((CHUNK, LANES), F32)
        for g in range(N_SSM_GROUPS):
            diag_h = jnp.where(grp == g, cb_diag[g], diag_h)
        coef = dskip_ref[...] + diag_h * dt
        scale = jnp.exp(acs)
        wgt = jnp.exp(tot - pre) * dt
        dec = jnp.exp(jnp.broadcast_to(tot, (BF16_ROWS, LANES)))
        pk_scale = _pack_hilo(scale, lane)
        scale_f = _dot(pk_scale, ef)
        scale_b = _dot(pk_scale, eb)
        coef_x = _dot(_pack_hilo(coef, lane), eb)
        pk_wd = _pack_hilo(jnp.concatenate([wgt, dec], axis=0),
                           lax.broadcasted_iota(jnp.int32, (CHUNK + BF16_ROWS, LANES), 1))
        wd_x = _dot(pk_wd, ef)
        xf = xb.astype(F32)
        y_part = y_diag + y_off_b * scale_b + coef_x * xf
        xw = (xf * wd_x[:CHUNK]).astype(BF16)
        upd = [_dot_tn(b_ref[0, rows, g * D_STATE:(g + 1) * D_STATE], xw[:, g * GROUP_W:(g + 1) * GROUP_W])
               for g in range(N_SSM_GROUPS)]
        return y_part, scale_f, wd_x[CHUNK:CHUNK + 1], upd

    cps = x_ref.shape[1] // CHUNK
    parts = [state_free_part(u) for u in range(cps)]
    for u in range(cps):
        rows = slice(u * CHUNK, (u + 1) * CHUNK)
        y_part, scale_f, decx, upd = parts[u]
        for g in range(N_SSM_GROUPS):
            cols = slice(g * GROUP_W, (g + 1) * GROUP_W)
            cg = c_ref[0, rows, g * D_STATE:(g + 1) * D_STATE]
            y_off_f = _dot(cg, h_scr[:, cols].astype(BF16))
            yg = (y_part[:, cols] + y_off_f * scale_f[:, cols]) * z_ref[0, rows, cols].astype(F32)
            ms = jnp.mean(yg * yg, axis=-1, keepdims=True)
            y_ref[0, rows, cols] = (yg * lax.rsqrt(ms + RMS_EPS) * nw_ref[:, cols]).astype(BF16)
            h_scr[:, cols] = h_scr[:, cols] * decx[:, cols] + upd[g]


def _ssd_fwd_call(xs, bm, cm, dt, zsil, hb, alog, dskip, nw, tril, ef, eb):
    b, s, _ = xs.shape
    cps = min(SSD_FWD_CHUNKS_PER_STEP, s // CHUNK)
    assert s % (cps * CHUNK) == 0
    nc = s // (cps * CHUNK)
    fwd = lambda i, c: (i, c, 0)
    const = lambda i, c: (0, 0)
    return pl.pallas_call(
        _ssd_fwd_kernel,
        grid=(b, nc),
        in_specs=[
            pl.BlockSpec((1, cps * CHUNK, D_INNER), fwd),
            pl.BlockSpec((1, cps * CHUNK, BC_W), fwd),
            pl.BlockSpec((1, cps * CHUNK, BC_W), fwd),
            pl.BlockSpec((1, cps * CHUNK, LANES), fwd),
            pl.BlockSpec((1, cps * CHUNK, D_INNER), fwd),
            pl.BlockSpec((1, cps, D_STATE, D_INNER), lambda i, c: (i, c, 0, 0)),
            pl.BlockSpec(alog.shape, const),
            pl.BlockSpec(dskip.shape, const),
            pl.BlockSpec(nw.shape, const),
            pl.BlockSpec(tril.shape, const),
            pl.BlockSpec(ef.shape, const),
            pl.BlockSpec(eb.shape, const),
        ],
        out_specs=pl.BlockSpec((1, cps * CHUNK, D_INNER), fwd),
        out_shape=jax.ShapeDtypeStruct((b, s, D_INNER), BF16),
        scratch_shapes=[pltpu.VMEM((D_STATE, D_INNER), F32)],
        compiler_params=_cparams(("parallel", "arbitrary")),
        name="ssd_fwd",
    )(xs, bm, cm, dt, zsil, hb, alog, dskip, nw, tril, ef, eb)


def _out_kernel(x_ref, att_ref, y_ref, gate_ref, wa_ref, ws_ref, wo_ref, lg_ref, lb_ref, o_ref):
    att = _dot(att_ref[0], wa_ref[...])
    ssm = _dot(y_ref[0], ws_ref[...])
    ga = gate_ref[0, :, :D_MODEL].astype(F32)
    gs = gate_ref[0, :, D_MODEL:].astype(F32)
    mixed = ga * att + gs * ssm
    out = _dot(mixed.astype(BF16), wo_ref[...])
    r = ALPHA * x_ref[0] + out
    mu = jnp.mean(r, axis=-1, keepdims=True)
    d = r - mu
    var = jnp.mean(d * d, axis=-1, keepdims=True)
    o_ref[0] = d * lax.rsqrt(var + LN_EPS) * lg_ref[...] + lb_ref[...]


def _out_call(x, att, y, gates, wa, ws, wo, lg, lb, tm):
    b, s, _ = x.shape
    tok = lambda i, j: (i, j, 0)
    const = lambda i, j: (0, 0)
    return pl.pallas_call(
        _out_kernel,
        grid=(b, s // tm),
        in_specs=[
            pl.BlockSpec((1, tm, D_MODEL), tok),
            pl.BlockSpec((1, tm, ATT_W), tok),
            pl.BlockSpec((1, tm, D_INNER), tok),
            pl.BlockSpec((1, tm, 2 * D_MODEL), tok),
            pl.BlockSpec(wa.shape, const),
            pl.BlockSpec(ws.shape, const),
            pl.BlockSpec(wo.shape, const),
            pl.BlockSpec(lg.shape, const),
            pl.BlockSpec(lb.shape, const),
        ],
        out_specs=pl.BlockSpec((1, tm, D_MODEL), tok),
        out_shape=jax.ShapeDtypeStruct((b, s, D_MODEL), F32),
        compiler_params=_cparams(("parallel", "parallel")),
        name="merge_out_ln",
    )(x, att, y, gates, wa, ws, wo, lg, lb)


def _rope_tables(seq_len):
    rows = seq_len // GRID_W
    row = jnp.repeat(jnp.arange(rows, dtype=F32), GRID_W)
    col = jnp.tile(jnp.arange(GRID_W, dtype=F32), rows)
    freqs = ROPE_THETA ** (-jnp.arange(0, AXIS_DIM, 2, dtype=F32) / AXIS_DIM)
    ang = jnp.concatenate([row[:, None] * freqs, col[:, None] * freqs], axis=-1)
    return jnp.cos(ang).T, jnp.sin(ang).T


def _expand_matrix(lane_offset):
    r = np.arange(LANES)[:, None] % (LANES // 2)
    head = np.arange(D_INNER)[None, :] // SSM_HEAD_DIM
    return jnp.asarray((r == head + lane_offset).astype(np.float32), dtype=BF16)


def _layer(x, p, tables):
    b, s, _ = x.shape
    tm = min(MAX_TOKEN_TILE, s)
    tq = min(MAX_QUERY_TILE, s)
    cos_t, sin_t = tables
    qt, k, vt = _qkv_call(x, p["wt_qkv"], cos_t[:, :s], sin_t[:, :s], p["qw"][:, :tm], p["kw"][:, :tm], tm)
    gsil, zsil, gates, dt = _gates_call(x, p["w_gates"], p["b_gate"], p["b_dt"], tm)
    xs, bm, cm = _xbc_call(x, p["w_xbc"], p["conv_w"], p["conv_b"], tm)
    att = _attn_call(qt, k, vt, gsil, tq)
    hb = _ssd_bwd_call(xs, bm, dt, p["alog"], p["tril"], p["eb"])
    y = _ssd_fwd_call(xs, bm, cm, dt, zsil, hb, p["alog"], p["dskip"], p["nw"], p["tril"], p["ef"], p["eb"])
    return _out_call(x, att, y, gates, p["wa"], p["ws"], p["wo"], p["lg"], p["lb"], tm)


def _prepare(w_in, b_gate, q_norm_w, k_norm_w, conv_w, conv_b, dt_bias_fwd, dt_bias_bwd,
             a_log_fwd, a_log_bwd, d_skip, ssm_norm_w, w_att_proj, w_ssm_proj, w_out, ln_g, ln_b):
    wq, wk, wv, wg, wz, wxbc, wdt, wgate = jnp.split(w_in, SPLIT_POINTS, axis=-1)
    perm = np.concatenate([np.arange(0, HEAD_DIM, 2), np.arange(1, HEAD_DIM, 2)])
    qperm = (np.arange(N_Q_HEADS)[:, None] * HEAD_DIM + perm[None, :]).reshape(-1)
    kperm = (np.arange(N_KV_HEADS)[:, None] * HEAD_DIM + perm[None, :]).reshape(-1)
    pad = LANES - 2 * N_SSM_HEADS
    zeros_h = jnp.zeros((N_SSM_HEADS,), F32)
    zeros_p = jnp.zeros((pad,), F32)
    return {
        "wt_qkv": jnp.concatenate([wq[:, qperm], wk[:, kperm], wv], axis=1).T.astype(BF16),
        "qw": jnp.broadcast_to(q_norm_w[perm][:, None], (HEAD_DIM, MAX_TOKEN_TILE)),
        "kw": jnp.broadcast_to(k_norm_w[perm][:, None], (HEAD_DIM, MAX_TOKEN_TILE)),
        "w_gates": jnp.concatenate([wg, wz, wgate, wdt, jnp.zeros((D_MODEL, pad), F32)], axis=1).astype(BF16),
        "w_xbc": wxbc.astype(BF16),
        "conv_w": conv_w,
        "conv_b": conv_b[None, :],
        "b_gate": b_gate[None, :],
        "b_dt": jnp.concatenate([dt_bias_fwd, dt_bias_bwd, zeros_p])[None, :],
        "alog": jnp.concatenate([a_log_fwd, a_log_bwd, zeros_p])[None, :],
        "dskip": jnp.concatenate([zeros_h, d_skip, zeros_p])[None, :],
        "nw": ssm_norm_w[None, :],
        "tril": jnp.asarray(np.tril(np.ones((CHUNK, CHUNK), np.float32)), dtype=BF16),
        "ef": _expand_matrix(0),
        "eb": _expand_matrix(N_SSM_HEADS),
        "wa": w_att_proj.astype(BF16),
        "ws": w_ssm_proj.astype(BF16),
        "wo": w_out.astype(BF16),
        "lg": ln_g[None, :],
        "lb": ln_b[None, :],
    }


def kernel(x_prompt, x_sample, w_in, b_gate, q_norm_w, k_norm_w, conv_w, conv_b, dt_bias_fwd,
           dt_bias_bwd, a_log_fwd, a_log_bwd, d_skip, ssm_norm_w, w_att_proj, w_ssm_proj, w_out,
           ln_g, ln_b):
    xp, xs = x_prompt, x_sample
    tables = _rope_tables(max(xp.shape[1], xs.shape[1]))
    for l in range(w_in.shape[0]):
        p = _prepare(w_in[l], b_gate[l], q_norm_w[l], k_norm_w[l], conv_w[l], conv_b[l],
                     dt_bias_fwd[l], dt_bias_bwd[l], a_log_fwd[l], a_log_bwd[l], d_skip[l],
                     ssm_norm_w[l], w_att_proj[l], w_ssm_proj[l], w_out[l], ln_g[l], ln_b[l])
        xp = _layer(xp, p, tables)
        xs = _layer(xs, p, tables)
    return (xp, xs)
```

```python
import numpy as np
import jax
import jax.numpy as jnp
from jax import lax
from jax.experimental import pallas as pl
from jax.experimental.pallas import tpu as pltpu

F32 = jnp.float32
BF16 = jnp.bfloat16

D_MODEL = 1024
GRID_W = 64
N_Q_HEADS = 16
N_KV_HEADS = 4
HEAD_DIM = 64
ATT_W = N_Q_HEADS * HEAD_DIM
KV_W = N_KV_HEADS * HEAD_DIM
AXIS_DIM = HEAD_DIM // 2
ROPE_THETA = 10000.0
D_INNER = 2 * D_MODEL
SSM_HEAD_DIM = 64
N_SSM_HEADS = D_INNER // SSM_HEAD_DIM
N_SSM_GROUPS = 4
D_STATE = 128
CONV_W = 5
CHUNK = 128
BC_W = N_SSM_GROUPS * D_STATE
CONV_CH = D_INNER + 2 * BC_W
RMS_EPS = 1e-6
LN_EPS = 1e-5
DEPTH = 1
ALPHA = (2 * DEPTH) ** 0.25
SPLITS = (ATT_W, KV_W, KV_W, ATT_W, D_INNER, CONV_CH, 2 * N_SSM_HEADS, 2 * D_MODEL)
SPLIT_POINTS = tuple(int(v) for v in np.cumsum(SPLITS)[:-1])

LANES = 128
HALO = 8
BF16_ROWS = 16
CONV_PHASES = 4
CONV_CHUNK = 512
SSD_BWD_CHUNKS_PER_STEP = 8
SSD_FWD_CHUNKS_PER_STEP = 4
GROUP_W = D_INNER // N_SSM_GROUPS
HEADS_PER_GROUP = N_SSM_HEADS // N_SSM_GROUPS
VMEM_LIMIT = 48 * 1024 * 1024
MAX_TOKEN_TILE = 512
MAX_QUERY_TILE = 1024
ATTN_KEY_BLOCK = 256
Q_SCALE = float(np.log2(np.e)) * HEAD_DIM ** -0.5


def _cparams(sem):
    return pltpu.CompilerParams(dimension_semantics=sem, vmem_limit_bytes=VMEM_LIMIT)


def _dot(a, b):
    return jnp.dot(a, b, preferred_element_type=F32)


def _dot_nt(a, b):
    return lax.dot_general(a, b, (((1,), (1,)), ((), ())), preferred_element_type=F32)


def _dot_tn(a, b):
    return lax.dot_general(a, b, (((0,), (0,)), ((), ())), preferred_element_type=F32)


def _silu(x):
    return x / (1.0 + jnp.exp(-x))


def _split2(v):
    hi = v.astype(BF16)
    lo = (v - hi.astype(F32)).astype(BF16)
    return hi, lo


def _split3(v):
    hi = v.astype(BF16)
    r = v - hi.astype(F32)
    mid = r.astype(BF16)
    lo = (r - mid.astype(F32)).astype(BF16)
    return hi, mid, lo


def _qkv_kernel(x_ref, wt_ref, cos_ref, sin_ref, qw_ref, kw_ref, qt_ref, k_ref, vt_ref):
    xb = x_ref[0].astype(BF16)
    cosv = cos_ref[...]
    sinv = sin_ref[...]
    half = HEAD_DIM // 2

    def norm_rope(acc, w_ref, scale):
        x0, x1 = acc[:half], acc[half:]
        ms = (jnp.sum(x0 * x0, axis=0, keepdims=True) + jnp.sum(x1 * x1, axis=0, keepdims=True)) * (1.0 / HEAD_DIM)
        inv = lax.rsqrt(ms + RMS_EPS)
        a0 = x0 * inv * w_ref[:half]
        a1 = x1 * inv * w_ref[half:]
        return (a0 * cosv - a1 * sinv) * scale, (a0 * sinv + a1 * cosv) * scale

    rows = 4 * HEAD_DIM
    for c in range(ATT_W // rows):
        acc = _dot_nt(wt_ref[c * rows:(c + 1) * rows, :], xb)
        for h in range(rows // HEAD_DIM):
            o0, o1 = norm_rope(acc[h * HEAD_DIM:(h + 1) * HEAD_DIM], qw_ref, Q_SCALE)
            r0 = c * rows + h * HEAD_DIM
            qt_ref[0, r0:r0 + half, :] = o0.astype(BF16)
            qt_ref[0, r0 + half:r0 + HEAD_DIM, :] = o1.astype(BF16)
    acc = _dot_nt(wt_ref[ATT_W:ATT_W + KV_W, :], xb)
    for g in range(N_KV_HEADS):
        o0, o1 = norm_rope(acc[g * HEAD_DIM:(g + 1) * HEAD_DIM], kw_ref, 1.0)
        k_ref[0, g] = jnp.concatenate([o0, o1], axis=0).T.astype(BF16)
    vt_ref[0] = _dot_nt(wt_ref[ATT_W + KV_W:ATT_W + 2 * KV_W, :], xb).astype(BF16)


def _qkv_call(x, wt_qkv, cos_t, sin_t, qw, kw, tm):
    b, s, _ = x.shape
    half = HEAD_DIM // 2
    return pl.pallas_call(
        _qkv_kernel,
        grid=(b, s // tm),
        in_specs=[
            pl.BlockSpec((1, tm, D_MODEL), lambda i, j: (i, j, 0)),
            pl.BlockSpec(wt_qkv.shape, lambda i, j: (0, 0)),
            pl.BlockSpec((half, tm), lambda i, j: (0, j)),
            pl.BlockSpec((half, tm), lambda i, j: (0, j)),
            pl.BlockSpec(qw.shape, lambda i, j: (0, 0)),
            pl.BlockSpec(kw.shape, lambda i, j: (0, 0)),
        ],
        out_specs=[
            pl.BlockSpec((1, ATT_W, tm), lambda i, j: (i, 0, j)),
            pl.BlockSpec((1, N_KV_HEADS, tm, HEAD_DIM), lambda i, j: (i, 0, j, 0)),
            pl.BlockSpec((1, KV_W, tm), lambda i, j: (i, 0, j)),
        ],
        out_shape=[
            jax.ShapeDtypeStruct((b, ATT_W, s), BF16),
            jax.ShapeDtypeStruct((b, N_KV_HEADS, s, HEAD_DIM), BF16),
            jax.ShapeDtypeStruct((b, KV_W, s), BF16),
        ],
        compiler_params=_cparams(("parallel", "parallel")),
        name="qkv_proj",
    )(x, wt_qkv, cos_t, sin_t, qw, kw)


def _gates_kernel(x_ref, w_ref, bg_ref, bdt_ref, g_ref, z_ref, gate_ref, dt_ref):
    xb = x_ref[0].astype(BF16)
    cw = 4 * LANES
    off = 0
    for c in range(ATT_W // cw):
        acc = _dot(xb, w_ref[:, off + c * cw:off + (c + 1) * cw])
        g_ref[0, :, c * cw:(c + 1) * cw] = _silu(acc).astype(BF16)
    off += ATT_W
    for c in range(D_INNER // cw):
        acc = _dot(xb, w_ref[:, off + c * cw:off + (c + 1) * cw])
        z_ref[0, :, c * cw:(c + 1) * cw] = _silu(acc).astype(BF16)
    off += D_INNER
    for c in range(2 * D_MODEL // cw):
        acc = _dot(xb, w_ref[:, off + c * cw:off + (c + 1) * cw]) + bg_ref[:, c * cw:(c + 1) * cw]
        gate_ref[0, :, c * cw:(c + 1) * cw] = (1.0 / (1.0 + jnp.exp(-acc))).astype(BF16)
    off += 2 * D_MODEL
    v = _dot(xb, w_ref[:, off:off + LANES]) + bdt_ref[...]
    dt_ref[0] = jnp.maximum(v, 0.0) + jnp.log1p(jnp.exp(-jnp.abs(v)))


def _gates_call(x, w_gates, b_gate, b_dt, tm):
    b, s, _ = x.shape
    tok = lambda i, j: (i, j, 0)
    const = lambda i, j: (0, 0)
    return pl.pallas_call(
        _gates_kernel,
        grid=(b, s // tm),
        in_specs=[
            pl.BlockSpec((1, tm, D_MODEL), tok),
            pl.BlockSpec(w_gates.shape, const),
            pl.BlockSpec(b_gate.shape, const),
            pl.BlockSpec(b_dt.shape, const),
        ],
        out_specs=[
            pl.BlockSpec((1, tm, ATT_W), tok),
            pl.BlockSpec((1, tm, D_INNER), tok),
            pl.BlockSpec((1, tm, 2 * D_MODEL), tok),
            pl.BlockSpec((1, tm, LANES), tok),
        ],
        out_shape=[
            jax.ShapeDtypeStruct((b, s, ATT_W), BF16),
            jax.ShapeDtypeStruct((b, s, D_INNER), BF16),
            jax.ShapeDtypeStruct((b, s, 2 * D_MODEL), BF16),
            jax.ShapeDtypeStruct((b, s, LANES), F32),
        ],
        compiler_params=_cparams(("parallel", "parallel")),
        name="gates_proj",
    )(x, w_gates, b_gate, b_dt)


def _xbc_kernel(x_ref, xp_ref, xn_ref, w_ref, cw_ref, cb_ref, xs_ref, b_ref, c_ref,
                acc_a, acc_b, y_a, y_b):
    tm = x_ref.shape[1]
    accs = (acc_a, acc_b)
    ys = (y_a, y_b)
    j = pl.program_id(1)
    nj = pl.num_programs(1)
    prev = jnp.where(j > 0, xp_ref[0], 0.0)
    nxt = jnp.where(j < nj - 1, xn_ref[0], 0.0)
    xall = jnp.concatenate([prev, x_ref[0], nxt], axis=0).astype(BF16)
    cw = CONV_CHUNK
    slabs = cw // LANES
    outs = ([(xs_ref, c * cw) for c in range(D_INNER // cw)] + [(b_ref, c * cw) for c in range(BC_W // cw)]
            + [(c_ref, c * cw) for c in range(BC_W // cw)])

    def project(c):
        acc = _dot(xall, w_ref[:, c * cw:(c + 1) * cw])
        for s in range(slabs):
            accs[c % 2][s] = acc[:, s * LANES:(s + 1) * LANES]

    project(0)
    for c, (o_ref, off) in enumerate(outs):
        if c + 1 < len(outs):
            project(c + 1)
        for s in range(slabs):
            col = c * cw + s * LANES
            for p in range(CONV_PHASES):
                y = cb_ref[:, col:col + LANES]
                for k in range(CONV_W):
                    r0 = HALO - CONV_W // 2 + k + p
                    tap = accs[c % 2][s, pl.ds(r0, tm // CONV_PHASES, stride=CONV_PHASES), :]
                    y = y + tap * cw_ref[k:k + 1, col:col + LANES]
                ys[c % 2][s, pl.ds(p, tm // CONV_PHASES, stride=CONV_PHASES), :] = _silu(y)
        o_ref[0, :, off:off + cw] = jnp.concatenate([ys[c % 2][s] for s in range(slabs)], axis=1).astype(BF16)


def _xbc_call(x, w_xbc, conv_w, conv_b, tm):
    b, s, _ = x.shape
    hb = tm // HALO
    last = s // HALO - 1
    cw = CONV_CHUNK
    return pl.pallas_call(
        _xbc_kernel,
        grid=(b, s // tm),
        in_specs=[
            pl.BlockSpec((1, tm, D_MODEL), lambda i, j: (i, j, 0)),
            pl.BlockSpec((1, HALO, D_MODEL), lambda i, j: (i, jnp.maximum(j * hb - 1, 0), 0)),
            pl.BlockSpec((1, HALO, D_MODEL), lambda i, j: (i, jnp.minimum((j + 1) * hb, last), 0)),
            pl.BlockSpec(w_xbc.shape, lambda i, j: (0, 0)),
            pl.BlockSpec(conv_w.shape, lambda i, j: (0, 0)),
            pl.BlockSpec(conv_b.shape, lambda i, j: (0, 0)),
        ],
        out_specs=[
            pl.BlockSpec((1, tm, D_INNER), lambda i, j: (i, j, 0)),
            pl.BlockSpec((1, tm, BC_W), lambda i, j: (i, j, 0)),
            pl.BlockSpec((1, tm, BC_W), lambda i, j: (i, j, 0)),
        ],
        out_shape=[
            jax.ShapeDtypeStruct((b, s, D_INNER), BF16),
            jax.ShapeDtypeStruct((b, s, BC_W), BF16),
            jax.ShapeDtypeStruct((b, s, BC_W), BF16),
        ],
        scratch_shapes=[pltpu.VMEM((cw // LANES, tm + 2 * HALO, LANES), F32),
                        pltpu.VMEM((cw // LANES, tm + 2 * HALO, LANES), F32),
                        pltpu.VMEM((cw // LANES, tm, LANES), F32),
                        pltpu.VMEM((cw // LANES, tm, LANES), F32)],
        compiler_params=_cparams(("parallel", "parallel")),
        name="xbc_conv_proj",
    )(x, x, x, w_xbc, conv_w, conv_b)


def _attn_kernel(qt_ref, k_ref, vt_ref, g_ref, o_ref):
    k = k_ref[0, 0]
    vt = vt_ref[0]
    vt_aug = jnp.concatenate([vt, jnp.ones((BF16_ROWS, vt.shape[1]), BF16)], axis=0)
    rep = N_Q_HEADS // N_KV_HEADS
    tq = qt_ref.shape[2]
    s_len = k.shape[0]
    kb = min(ATTN_KEY_BLOCK, s_len)
    nblk = s_len // kb

    qt = jnp.concatenate([qt_ref[0, r * HEAD_DIM:(r + 1) * HEAD_DIM, :] for r in range(rep)], axis=1)
    m = acc = None
    st_next = _dot(k[:kb], qt)
    for j in range(nblk):
        st = st_next
        if j + 1 < nblk:
            st_next = _dot(k[(j + 1) * kb:(j + 2) * kb], qt)
        bm = jnp.max(st, axis=0, keepdims=True)
        m_new = bm if m is None else jnp.maximum(m, bm)
        pt = jnp.exp2(st - m_new).astype(BF16)
        part = _dot(vt_aug[:, j * kb:(j + 1) * kb], pt)
        acc = part if acc is None else acc * jnp.exp2(m - m_new) + part
        m = m_new
    on = acc[:HEAD_DIM] / acc[HEAD_DIM:HEAD_DIM + 1]
    o = jnp.concatenate([on[:, r * tq:(r + 1) * tq].T for r in range(rep)], axis=1)
    o_ref[0] = (o * g_ref[0].astype(F32)).astype(BF16)


def _attn_call(qt, k, vt, gsil, tq):
    b, _, s = qt.shape
    gw = ATT_W // N_KV_HEADS
    return pl.pallas_call(
        _attn_kernel,
        grid=(b, N_KV_HEADS, s // tq),
        in_specs=[
            pl.BlockSpec((1, gw, tq), lambda i, g, j: (i, g, j)),
            pl.BlockSpec((1, 1, s, HEAD_DIM), lambda i, g, j: (i, g, 0, 0)),
            pl.BlockSpec((1, HEAD_DIM, s), lambda i, g, j: (i, g, 0)),
            pl.BlockSpec((1, tq, gw), lambda i, g, j: (i, j, g)),
        ],
        out_specs=pl.BlockSpec((1, tq, gw), lambda i, g, j: (i, j, g)),
        out_shape=jax.ShapeDtypeStruct((b, s, ATT_W), BF16),
        compiler_params=_cparams(("parallel", "parallel", "parallel")),
        name="attention",
    )(qt, k, vt, gsil)


def _pack_hilo(v, lane):
    hi = v.astype(BF16).astype(F32)
    lo = (v - hi).astype(BF16).astype(F32)
    return jnp.where(lane < LANES // 2, hi, pltpu.roll(lo, LANES // 2, 1)).astype(BF16)


def _chunk_decay_terms(dt, alog_ref, tril_ref):
    a = -jnp.exp(alog_ref[...])
    da = dt * a
    tril = tril_ref[...]
    hi, mid, lo = _split3(da)
    pre = _dot(tril, hi) + _dot(tril, mid) + _dot(tril, lo)
    tot = pre[CHUNK - 1:CHUNK, :]
    return da, pre, tot


def _ssd_bwd_kernel(x_ref, b_ref, dt_ref, alog_ref, tril_ref, eb_ref, hb_ref, h_scr):
    c = pl.program_id(1)

    @pl.when(c == 0)
    def _():
        h_scr[...] = jnp.zeros_like(h_scr)

    cps = x_ref.shape[1] // CHUNK
    terms = []
    for u in range(cps):
        rows = slice(u * CHUNK, (u + 1) * CHUNK)
        dt = dt_ref[0, rows, :]
        da, pre, tot = _chunk_decay_terms(dt, alog_ref, tril_ref)
        wgt = jnp.exp(pre - da) * dt
        dec = jnp.exp(jnp.broadcast_to(tot, (BF16_ROWS, LANES)))
        packed = _pack_hilo(jnp.concatenate([wgt, dec], axis=0),
                            lax.broadcasted_iota(jnp.int32, (CHUNK + BF16_ROWS, LANES), 1))
        ex = _dot(packed, eb_ref[...])
        xw = (x_ref[0, rows, :].astype(F32) * ex[:CHUNK]).astype(BF16)
        upd = [_dot_tn(b_ref[0, rows, g * D_STATE:(g + 1) * D_STATE], xw[:, g * GROUP_W:(g + 1) * GROUP_W])
               for g in range(N_SSM_GROUPS)]
        terms.append((ex[CHUNK:CHUNK + 1], upd))
    for u in reversed(range(cps)):
        decx, upd = terms[u]
        hb_ref[0, u] = h_scr[...].astype(BF16)
        for g in range(N_SSM_GROUPS):
            cols = slice(g * GROUP_W, (g + 1) * GROUP_W)
            h_scr[:, cols] = h_scr[:, cols] * decx[:, cols] + upd[g]


def _ssd_bwd_call(xs, bm, dt, alog, tril, eb):
    b, s, _ = xs.shape
    cps = min(SSD_BWD_CHUNKS_PER_STEP, s // CHUNK)
    assert s % (cps * CHUNK) == 0
    nc = s // (cps * CHUNK)
    rev = lambda i, c: (i, nc - 1 - c, 0)
    return pl.pallas_call(
        _ssd_bwd_kernel,
        grid=(b, nc),
        in_specs=[
            pl.BlockSpec((1, cps * CHUNK, D_INNER), rev),
            pl.BlockSpec((1, cps * CHUNK, BC_W), rev),
            pl.BlockSpec((1, cps * CHUNK, LANES), rev),
            pl.BlockSpec(alog.shape, lambda i, c: (0, 0)),
            pl.BlockSpec(tril.shape, lambda i, c: (0, 0)),
            pl.BlockSpec(eb.shape, lambda i, c: (0, 0)),
        ],
        out_specs=pl.BlockSpec((1, cps, D_STATE, D_INNER), lambda i, c: (i, nc - 1 - c, 0, 0)),
        out_shape=jax.ShapeDtypeStruct((b, s // CHUNK, D_STATE, D_INNER), BF16),
        scratch_shapes=[pltpu.VMEM((D_STATE, D_INNER), F32)],
        compiler_params=_cparams(("parallel", "arbitrary")),
        name="ssd_bwd_states",
    )(xs, bm, dt, alog, tril, eb)


def _ssd_fwd_kernel(x_ref, b_ref, c_ref, dt_ref, z_ref, hb_ref, alog_ref, dskip_ref, nw_ref,
                    tril_ref, ef_ref, eb_ref, sel_ref, y_ref, h_scr):
    c = pl.program_id(1)

    @pl.when(c == 0)
    def _():
        h_scr[...] = jnp.zeros_like(h_scr)

    half = N_SSM_HEADS
    lane = lax.broadcasted_iota(jnp.int32, (CHUNK, LANES), 1)
    row = lax.broadcasted_iota(jnp.int32, (CHUNK, CHUNK), 0)
    col = lax.broadcasted_iota(jnp.int32, (CHUNK, CHUNK), 1)
    lower = col <= row
    eye = col == row
    first_head = lane < SSM_HEAD_DIM
    grp = (lane - half) >> 3
    ef = ef_ref[...]
    eb = eb_ref[...]

    def state_free_part(u):
        rows = slice(u * CHUNK, (u + 1) * CHUNK)
        dt = dt_ref[0, rows, :]
        da, pre, tot = _chunk_decay_terms(dt, alog_ref, tril_ref)
        acs = jnp.where(lane < half, pre, tot - pre + da)
        acsp_t = (acs - jnp.log(dt)).T
        cols = _dot(jnp.concatenate(_split2(acs), axis=1), sel_ref[...])
        xb = x_ref[0, rows, :]
        hb = hb_ref[0, u]
        y_diag = []
        y_off_b = []
        cb_diag = []
        for g in range(N_SSM_GROUPS):
            bg = b_ref[0, rows, g * D_STATE:(g + 1) * D_STATE]
            cg = c_ref[0, rows, g * D_STATE:(g + 1) * D_STATE]
            cb = _dot_nt(cg, bg)
            cb_diag.append(jnp.sum(jnp.where(eye, cb, 0.0), axis=-1, keepdims=True))
            for k in range(HEADS_PER_GROUP // 2):
                pair = g * (HEADS_PER_GROUP // 2) + k
                col_b = cols[:, pair * 2 * CHUNK:(pair + 1) * 2 * CHUNK]
                ms = []
                for t in range(2):
                    h = 2 * pair + t
                    arg = jnp.where(lower,
                                    acs[:, h:h + 1] - acsp_t[h:h + 1, :],
                                    col_b[:, t * CHUNK:(t + 1) * CHUNK] - acsp_t[half + h:half + h + 1, :])
                    ms.append((cb * jnp.exp(arg)).astype(BF16))
                xp = xb[:, pair * LANES:(pair + 1) * LANES]
                zero = jnp.zeros_like(xp)
                rhs = jnp.concatenate([jnp.where(first_head, xp, zero), jnp.where(first_head, zero, xp)], axis=0)
                y_diag.append(_dot(jnp.concatenate(ms, axis=1), rhs))
            y_off_b.append(_dot(cg, hb[:, g * GROUP_W:(g + 1) * GROUP_W]))
        y_diag = jnp.concatenate(y_diag, axis=1)
        y_off_b = jnp.concatenate(y_off_b, axis=1)

        diag_h = jnp.zeros((CHUNK, LANES), F32)
        for g in range(N_SSM_GROUPS):
            diag_h = jnp.where(grp == g, cb_diag[g], diag_h)
        coef = dskip_ref[...] + diag_h * dt
        scale = jnp.exp(acs)
        wgt = jnp.exp(tot - pre) * dt
        dec = jnp.exp(jnp.broadcast_to(tot, (BF16_ROWS, LANES)))
        pk_scale = _pack_hilo(scale, lane)
        scale_f = _dot(pk_scale, ef)
        scale_b = _dot(pk_scale, eb)
        coef_x = _dot(_pack_hilo(coef, lane), eb)
        pk_wd = _pack_hilo(jnp.concatenate([wgt, dec], axis=0),
                           lax.broadcasted_iota(jnp.int32, (CHUNK + BF16_ROWS, LANES), 1))
        wd_x = _dot(pk_wd, ef)
        xf = xb.astype(F32)
        y_part = y_diag + y_off_b * scale_b + coef_x * xf
        xw = (xf * wd_x[:CHUNK]).astype(BF16)
        upd = [_dot_tn(b_ref[0, rows, g * D_STATE:(g + 1) * D_STATE], xw[:, g * GROUP_W:(g + 1) * GROUP_W])
               for g in range(N_SSM_GROUPS)]
        return y_part, scale_f, wd_x[CHUNK:CHUNK + 1], upd

    cps = x_ref.shape[1] // CHUNK
    parts = [state_free_part(u) for u in range(cps)]
    for u in range(cps):
        rows = slice(u * CHUNK, (u + 1) * CHUNK)
        y_part, scale_f, decx, upd = parts[u]
        for g in range(N_SSM_GROUPS):
            cols = slice(g * GROUP_W, (g + 1) * GROUP_W)
            cg = c_ref[0, rows, g * D_STATE:(g + 1) * D_STATE]
            y_off_f = _dot(cg, h_scr[:, cols].astype(BF16))
            yg = (y_part[:, cols] + y_off_f * scale_f[:, cols]) * z_ref[0, rows, cols].astype(F32)
            ms = jnp.mean(yg * yg, axis=-1, keepdims=True)
            y_ref[0, rows, cols] = (yg * lax.rsqrt(ms + RMS_EPS) * nw_ref[:, cols]).astype(BF16)
            h_scr[:, cols] = h_scr[:, cols] * decx[:, cols] + upd[g]


def _ssd_fwd_call(xs, bm, cm, dt, zsil, hb, alog, dskip, nw, tril, ef, eb, sel):
    b, s, _ = xs.shape
    cps = min(SSD_FWD_CHUNKS_PER_STEP, s // CHUNK)
    assert s % (cps * CHUNK) == 0
    nc = s // (cps * CHUNK)
    fwd = lambda i, c: (i, c, 0)
    const = lambda i, c: (0, 0)
    return pl.pallas_call(
        _ssd_fwd_kernel,
        grid=(b, nc),
        in_specs=[
            pl.BlockSpec((1, cps * CHUNK, D_INNER), fwd),
            pl.BlockSpec((1, cps * CHUNK, BC_W), fwd),
            pl.BlockSpec((1, cps * CHUNK, BC_W), fwd),
            pl.BlockSpec((1, cps * CHUNK, LANES), fwd),
            pl.BlockSpec((1, cps * CHUNK, D_INNER), fwd),
            pl.BlockSpec((1, cps, D_STATE, D_INNER), lambda i, c: (i, c, 0, 0)),
            pl.BlockSpec(alog.shape, const),
            pl.BlockSpec(dskip.shape, const),
            pl.BlockSpec(nw.shape, const),
            pl.BlockSpec(tril.shape, const),
            pl.BlockSpec(ef.shape, const),
            pl.BlockSpec(eb.shape, const),
            pl.BlockSpec(sel.shape, const),
        ],
        out_specs=pl.BlockSpec((1, cps * CHUNK, D_INNER), fwd),
        out_shape=jax.ShapeDtypeStruct((b, s, D_INNER), BF16),
        scratch_shapes=[pltpu.VMEM((D_STATE, D_INNER), F32)],
        compiler_params=_cparams(("parallel", "arbitrary")),
        name="ssd_fwd",
    )(xs, bm, cm, dt, zsil, hb, alog, dskip, nw, tril, ef, eb, sel)


def _out_kernel(x_ref, att_ref, y_ref, gate_ref, wa_ref, ws_ref, wo_ref, lg_ref, lb_ref, o_ref):
    att = _dot(att_ref[0], wa_ref[...])
    ssm = _dot(y_ref[0], ws_ref[...])
    ga = gate_ref[0, :, :D_MODEL].astype(F32)
    gs = gate_ref[0, :, D_MODEL:].astype(F32)
    mixed = ga * att + gs * ssm
    out = _dot(mixed.astype(BF16), wo_ref[...])
    r = ALPHA * x_ref[0] + out
    mu = jnp.mean(r, axis=-1, keepdims=True)
    d = r - mu
    var = jnp.mean(d * d, axis=-1, keepdims=True)
    o_ref[0] = d * lax.rsqrt(var + LN_EPS) * lg_ref[...] + lb_ref[...]


def _out_call(x, att, y, gates, wa, ws, wo, lg, lb, tm):
    b, s, _ = x.shape
    tok = lambda i, j: (i, j, 0)
    const = lambda i, j: (0, 0)
    return pl.pallas_call(
        _out_kernel,
        grid=(b, s // tm),
        in_specs=[
            pl.BlockSpec((1, tm, D_MODEL), tok),
            pl.BlockSpec((1, tm, ATT_W), tok),
            pl.BlockSpec((1, tm, D_INNER), tok),
            pl.BlockSpec((1, tm, 2 * D_MODEL), tok),
            pl.BlockSpec(wa.shape, const),
            pl.BlockSpec(ws.shape, const),
            pl.BlockSpec(wo.shape, const),
            pl.BlockSpec(lg.shape, const),
            pl.BlockSpec(lb.shape, const),
        ],
        out_specs=pl.BlockSpec((1, tm, D_MODEL), tok),
        out_shape=jax.ShapeDtypeStruct((b, s, D_MODEL), F32),
        compiler_params=_cparams(("parallel", "parallel")),
        name="merge_out_ln",
    )(x, att, y, gates, wa, ws, wo, lg, lb)


def _rope_tables(seq_len):
    rows = seq_len // GRID_W
    row = jnp.repeat(jnp.arange(rows, dtype=F32), GRID_W)
    col = jnp.tile(jnp.arange(GRID_W, dtype=F32), rows)
    freqs = ROPE_THETA ** (-jnp.arange(0, AXIS_DIM, 2, dtype=F32) / AXIS_DIM)
    ang = jnp.concatenate([row[:, None] * freqs, col[:, None] * freqs], axis=-1)
    return jnp.cos(ang).T, jnp.sin(ang).T


def _expand_matrix(lane_offset):
    r = np.arange(LANES)[:, None] % (LANES // 2)
    head = np.arange(D_INNER)[None, :] // SSM_HEAD_DIM
    return jnp.asarray((r == head + lane_offset).astype(np.float32), dtype=BF16)


def _select_matrices():
    lane = np.arange(2 * LANES)[:, None] % LANES
    second = (np.arange(2 * CHUNK)[None, :] >= CHUNK).astype(np.int64)
    mats = [(lane == N_SSM_HEADS + 2 * p + second) for p in range(N_SSM_HEADS // 2)]
    return jnp.asarray(np.concatenate(mats, axis=1).astype(np.float32), dtype=BF16)


def _layer(x, p, tables):
    b, s, _ = x.shape
    tm = min(MAX_TOKEN_TILE, s)
    tq = min(MAX_QUERY_TILE, s)
    cos_t, sin_t = tables
    qt, k, vt = _qkv_call(x, p["wt_qkv"], cos_t[:, :s], sin_t[:, :s], p["qw"][:, :tm], p["kw"][:, :tm], tm)
    gsil, zsil, gates, dt = _gates_call(x, p["w_gates"], p["b_gate"], p["b_dt"], tm)
    xs, bm, cm = _xbc_call(x, p["w_xbc"], p["conv_w"], p["conv_b"], tm)
    att = _attn_call(qt, k, vt, gsil, tq)
    hb = _ssd_bwd_call(xs, bm, dt, p["alog"], p["tril"], p["eb"])
    y = _ssd_fwd_call(xs, bm, cm, dt, zsil, hb, p["alog"], p["dskip"], p["nw"], p["tril"], p["ef"], p["eb"],
                      p["sel"])
    return _out_call(x, att, y, gates, p["wa"], p["ws"], p["wo"], p["lg"], p["lb"], tm)


def _prepare(w_in, b_gate, q_norm_w, k_norm_w, conv_w, conv_b, dt_bias_fwd, dt_bias_bwd,
             a_log_fwd, a_log_bwd, d_skip, ssm_norm_w, w_att_proj, w_ssm_proj, w_out, ln_g, ln_b):
    wq, wk, wv, wg, wz, wxbc, wdt, wgate = jnp.split(w_in, SPLIT_POINTS, axis=-1)
    perm = np.concatenate([np.arange(0, HEAD_DIM, 2), np.arange(1, HEAD_DIM, 2)])
    qperm = (np.arange(N_Q_HEADS)[:, None] * HEAD_DIM + perm[None, :]).reshape(-1)
    kperm = (np.arange(N_KV_HEADS)[:, None] * HEAD_DIM + perm[None, :]).reshape(-1)
    pad = LANES - 2 * N_SSM_HEADS
    zeros_h = jnp.zeros((N_SSM_HEADS,), F32)
    zeros_p = jnp.zeros((pad,), F32)
    return {
        "wt_qkv": jnp.concatenate([wq[:, qperm], wk[:, kperm], wv], axis=1).T.astype(BF16),
        "qw": jnp.broadcast_to(q_norm_w[perm][:, None], (HEAD_DIM, MAX_TOKEN_TILE)),
        "kw": jnp.broadcast_to(k_norm_w[perm][:, None], (HEAD_DIM, MAX_TOKEN_TILE)),
        "w_gates": jnp.concatenate([wg, wz, wgate, wdt, jnp.zeros((D_MODEL, pad), F32)], axis=1).astype(BF16),
        "w_xbc": wxbc.astype(BF16),
        "conv_w": conv_w,
        "conv_b": conv_b[None, :],
        "b_gate": b_gate[None, :],
        "b_dt": jnp.concatenate([dt_bias_fwd, dt_bias_bwd, zeros_p])[None, :],
        "alog": jnp.concatenate([a_log_fwd, a_log_bwd, zeros_p])[None, :],
        "dskip": jnp.concatenate([zeros_h, d_skip, zeros_p])[None, :],
        "nw": ssm_norm_w[None, :],
        "tril": jnp.asarray(np.tril(np.ones((CHUNK, CHUNK), np.float32)), dtype=BF16),
        "ef": _expand_matrix(0),
        "eb": _expand_matrix(N_SSM_HEADS),
        "sel": _select_matrices(),
        "wa": w_att_proj.astype(BF16),
        "ws": w_ssm_proj.astype(BF16),
        "wo": w_out.astype(BF16),
        "lg": ln_g[None, :],
        "lb": ln_b[None, :],
    }


def kernel(x_prompt, x_sample, w_in, b_gate, q_norm_w, k_norm_w, conv_w, conv_b, dt_bias_fwd,
           dt_bias_bwd, a_log_fwd, a_log_bwd, d_skip, ssm_norm_w, w_att_proj, w_ssm_proj, w_out,
           ln_g, ln_b):
    xp, xs = x_prompt, x_sample
    tables = _rope_tables(max(xp.shape[1], xs.shape[1]))
    for l in range(w_in.shape[0]):
        p = _prepare(w_in[l], b_gate[l], q_norm_w[l], k_norm_w[l], conv_w[l], conv_b[l],
                     dt_bias_fwd[l], dt_bias_bwd[l], a_log_fwd[l], a_log_bwd[l], d_skip[l],
                     ssm_norm_w[l], w_att_proj[l], w_ssm_proj[l], w_out[l], ln_g[l], ln_b[l])
        xp = _layer(xp, p, tables)
        xs = _layer(xs, p, tables)
    return (xp, xs)
```

```python
import numpy as np
import jax
import jax.numpy as jnp
from jax import lax
from jax.experimental import pallas as pl
from jax.experimental.pallas import tpu as pltpu

F32 = jnp.float32
BF16 = jnp.bfloat16

D_MODEL = 1024
GRID_W = 64
N_Q_HEADS = 16
N_KV_HEADS = 4
HEAD_DIM = 64
ATT_W = N_Q_HEADS * HEAD_DIM
KV_W = N_KV_HEADS * HEAD_DIM
AXIS_DIM = HEAD_DIM // 2
ROPE_THETA = 10000.0
D_INNER = 2 * D_MODEL
SSM_HEAD_DIM = 64
N_SSM_HEADS = D_INNER // SSM_HEAD_DIM
N_SSM_GROUPS = 4
D_STATE = 128
CONV_W = 5
CHUNK = 128
BC_W = N_SSM_GROUPS * D_STATE
CONV_CH = D_INNER + 2 * BC_W
RMS_EPS = 1e-6
LN_EPS = 1e-5
DEPTH = 1
ALPHA = (2 * DEPTH) ** 0.25
SPLITS = (ATT_W, KV_W, KV_W, ATT_W, D_INNER, CONV_CH, 2 * N_SSM_HEADS, 2 * D_MODEL)
SPLIT_POINTS = tuple(int(v) for v in np.cumsum(SPLITS)[:-1])

LANES = 128
HALO = 8
BF16_ROWS = 16
CONV_PHASES = 4
CONV_CHUNK = 512
SSD_BWD_CHUNKS_PER_STEP = 8
SSD_FWD_CHUNKS_PER_STEP = 4
GROUP_W = D_INNER // N_SSM_GROUPS
HEADS_PER_GROUP = N_SSM_HEADS // N_SSM_GROUPS
VMEM_LIMIT = 48 * 1024 * 1024
MAX_TOKEN_TILE = 512
GATES_TOKEN_TILE = 256
MAX_QUERY_TILE = 1024
ATTN_KEY_BLOCK = 256
Q_SCALE = float(np.log2(np.e)) * HEAD_DIM ** -0.5


def _cparams(sem):
    return pltpu.CompilerParams(dimension_semantics=sem, vmem_limit_bytes=VMEM_LIMIT)


def _dot(a, b):
    return jnp.dot(a, b, preferred_element_type=F32)


def _dot_nt(a, b):
    return lax.dot_general(a, b, (((1,), (1,)), ((), ())), preferred_element_type=F32)


def _dot_tn(a, b):
    return lax.dot_general(a, b, (((0,), (0,)), ((), ())), preferred_element_type=F32)


def _silu(x):
    return x / (1.0 + jnp.exp(-x))


def _split2(v):
    hi = v.astype(BF16)
    lo = (v - hi.astype(F32)).astype(BF16)
    return hi, lo


def _split3(v):
    hi = v.astype(BF16)
    r = v - hi.astype(F32)
    mid = r.astype(BF16)
    lo = (r - mid.astype(F32)).astype(BF16)
    return hi, mid, lo


def _qkv_kernel(x_ref, wt_ref, cos_ref, sin_ref, qw_ref, kw_ref, qt_ref, k_ref, vt_ref):
    xb = x_ref[0].astype(BF16)
    cosv = cos_ref[...]
    sinv = sin_ref[...]
    half = HEAD_DIM // 2

    def norm_rope(acc, w_ref, scale):
        x0, x1 = acc[:half], acc[half:]
        ms = (jnp.sum(x0 * x0, axis=0, keepdims=True) + jnp.sum(x1 * x1, axis=0, keepdims=True)) * (1.0 / HEAD_DIM)
        inv = lax.rsqrt(ms + RMS_EPS)
        a0 = x0 * inv * w_ref[:half]
        a1 = x1 * inv * w_ref[half:]
        return (a0 * cosv - a1 * sinv) * scale, (a0 * sinv + a1 * cosv) * scale

    rows = 4 * HEAD_DIM
    for c in range(ATT_W // rows):
        acc = _dot_nt(wt_ref[c * rows:(c + 1) * rows, :], xb)
        for h in range(rows // HEAD_DIM):
            o0, o1 = norm_rope(acc[h * HEAD_DIM:(h + 1) * HEAD_DIM], qw_ref, Q_SCALE)
            r0 = c * rows + h * HEAD_DIM
            qt_ref[0, r0:r0 + half, :] = o0.astype(BF16)
            qt_ref[0, r0 + half:r0 + HEAD_DIM, :] = o1.astype(BF16)
    acc = _dot_nt(wt_ref[ATT_W:ATT_W + KV_W, :], xb)
    for g in range(N_KV_HEADS):
        o0, o1 = norm_rope(acc[g * HEAD_DIM:(g + 1) * HEAD_DIM], kw_ref, 1.0)
        k_ref[0, g] = jnp.concatenate([o0, o1], axis=0).T.astype(BF16)
    vt_ref[0] = _dot_nt(wt_ref[ATT_W + KV_W:ATT_W + 2 * KV_W, :], xb).astype(BF16)


def _qkv_call(x, wt_qkv, cos_t, sin_t, qw, kw, tm):
    b, s, _ = x.shape
    half = HEAD_DIM // 2
    return pl.pallas_call(
        _qkv_kernel,
        grid=(b, s // tm),
        in_specs=[
            pl.BlockSpec((1, tm, D_MODEL), lambda i, j: (i, j, 0)),
            pl.BlockSpec(wt_qkv.shape, lambda i, j: (0, 0)),
            pl.BlockSpec((half, tm), lambda i, j: (0, j)),
            pl.BlockSpec((half, tm), lambda i, j: (0, j)),
            pl.BlockSpec(qw.shape, lambda i, j: (0, 0)),
            pl.BlockSpec(kw.shape, lambda i, j: (0, 0)),
        ],
        out_specs=[
            pl.BlockSpec((1, ATT_W, tm), lambda i, j: (i, 0, j)),
            pl.BlockSpec((1, N_KV_HEADS, tm, HEAD_DIM), lambda i, j: (i, 0, j, 0)),
            pl.BlockSpec((1, KV_W, tm), lambda i, j: (i, 0, j)),
        ],
        out_shape=[
            jax.ShapeDtypeStruct((b, ATT_W, s), BF16),
            jax.ShapeDtypeStruct((b, N_KV_HEADS, s, HEAD_DIM), BF16),
            jax.ShapeDtypeStruct((b, KV_W, s), BF16),
        ],
        compiler_params=_cparams(("parallel", "parallel")),
        name="qkv_proj",
    )(x, wt_qkv, cos_t, sin_t, qw, kw)


def _gates_kernel(x_ref, w_ref, bg_ref, bdt_ref, g_ref, z_ref, gate_ref, dt_ref):
    xb = x_ref[0].astype(BF16)
    cw = 4 * LANES
    off = 0
    for c in range(ATT_W // cw):
        acc = _dot(xb, w_ref[:, off + c * cw:off + (c + 1) * cw])
        g_ref[0, :, c * cw:(c + 1) * cw] = _silu(acc).astype(BF16)
    off += ATT_W
    for c in range(D_INNER // cw):
        acc = _dot(xb, w_ref[:, off + c * cw:off + (c + 1) * cw])
        z_ref[0, :, c * cw:(c + 1) * cw] = _silu(acc).astype(BF16)
    off += D_INNER
    for c in range(2 * D_MODEL // cw):
        acc = _dot(xb, w_ref[:, off + c * cw:off + (c + 1) * cw]) + bg_ref[:, c * cw:(c + 1) * cw]
        gate_ref[0, :, c * cw:(c + 1) * cw] = (1.0 / (1.0 + jnp.exp(-acc))).astype(BF16)
    off += 2 * D_MODEL
    v = _dot(xb, w_ref[:, off:off + LANES]) + bdt_ref[...]
    dt_ref[0] = jnp.maximum(v, 0.0) + jnp.log1p(jnp.exp(-jnp.abs(v)))


def _gates_call(x, w_gates, b_gate, b_dt, tm):
    b, s, _ = x.shape
    tok = lambda i, j: (i, j, 0)
    const = lambda i, j: (0, 0)
    return pl.pallas_call(
        _gates_kernel,
        grid=(b, s // tm),
        in_specs=[
            pl.BlockSpec((1, tm, D_MODEL), tok),
            pl.BlockSpec(w_gates.shape, const),
            pl.BlockSpec(b_gate.shape, const),
            pl.BlockSpec(b_dt.shape, const),
        ],
        out_specs=[
            pl.BlockSpec((1, tm, ATT_W), tok),
            pl.BlockSpec((1, tm, D_INNER), tok),
            pl.BlockSpec((1, tm, 2 * D_MODEL), tok),
            pl.BlockSpec((1, tm, LANES), tok),
        ],
        out_shape=[
            jax.ShapeDtypeStruct((b, s, ATT_W), BF16),
            jax.ShapeDtypeStruct((b, s, D_INNER), BF16),
            jax.ShapeDtypeStruct((b, s, 2 * D_MODEL), BF16),
            jax.ShapeDtypeStruct((b, s, LANES), F32),
        ],
        compiler_params=_cparams(("parallel", "parallel")),
        name="gates_proj",
    )(x, w_gates, b_gate, b_dt)


def _xbc_kernel(x_ref, xp_ref, xn_ref, w_ref, cw_ref, cb_ref, xs_ref, b_ref, c_ref,
                acc_a, acc_b, y_a, y_b):
    tm = x_ref.shape[1]
    accs = (acc_a, acc_b)
    ys = (y_a, y_b)
    j = pl.program_id(1)
    nj = pl.num_programs(1)
    prev = jnp.where(j > 0, xp_ref[0], 0.0)
    nxt = jnp.where(j < nj - 1, xn_ref[0], 0.0)
    xall = jnp.concatenate([prev, x_ref[0], nxt], axis=0).astype(BF16)
    cw = CONV_CHUNK
    slabs = cw // LANES
    outs = ([(xs_ref, c * cw) for c in range(D_INNER // cw)] + [(b_ref, c * cw) for c in range(BC_W // cw)]
            + [(c_ref, c * cw) for c in range(BC_W // cw)])

    def project(c):
        acc = _dot(xall, w_ref[:, c * cw:(c + 1) * cw])
        for s in range(slabs):
            accs[c % 2][s] = acc[:, s * LANES:(s + 1) * LANES]

    project(0)
    for c, (o_ref, off) in enumerate(outs):
        if c + 1 < len(outs):
            project(c + 1)
        for s in range(slabs):
            col = c * cw + s * LANES
            for p in range(CONV_PHASES):
                y = cb_ref[:, col:col + LANES]
                for k in range(CONV_W):
                    r0 = HALO - CONV_W // 2 + k + p
                    tap = accs[c % 2][s, pl.ds(r0, tm // CONV_PHASES, stride=CONV_PHASES), :]
                    y = y + tap * cw_ref[k:k + 1, col:col + LANES]
                ys[c % 2][s, pl.ds(p, tm // CONV_PHASES, stride=CONV_PHASES), :] = _silu(y)
        o_ref[0, :, off:off + cw] = jnp.concatenate([ys[c % 2][s] for s in range(slabs)], axis=1).astype(BF16)


def _xbc_call(x, w_xbc, conv_w, conv_b, tm):
    b, s, _ = x.shape
    hb = tm // HALO
    last = s // HALO - 1
    cw = CONV_CHUNK
    return pl.pallas_call(
        _xbc_kernel,
        grid=(b, s // tm),
        in_specs=[
            pl.BlockSpec((1, tm, D_MODEL), lambda i, j: (i, j, 0)),
            pl.BlockSpec((1, HALO, D_MODEL), lambda i, j: (i, jnp.maximum(j * hb - 1, 0), 0)),
            pl.BlockSpec((1, HALO, D_MODEL), lambda i, j: (i, jnp.minimum((j + 1) * hb, last), 0)),
            pl.BlockSpec(w_xbc.shape, lambda i, j: (0, 0)),
            pl.BlockSpec(conv_w.shape, lambda i, j: (0, 0)),
            pl.BlockSpec(conv_b.shape, lambda i, j: (0, 0)),
        ],
        out_specs=[
            pl.BlockSpec((1, tm, D_INNER), lambda i, j: (i, j, 0)),
            pl.BlockSpec((1, tm, BC_W), lambda i, j: (i, j, 0)),
            pl.BlockSpec((1, tm, BC_W), lambda i, j: (i, j, 0)),
        ],
        out_shape=[
            jax.ShapeDtypeStruct((b, s, D_INNER), BF16),
            jax.ShapeDtypeStruct((b, s, BC_W), BF16),
            jax.ShapeDtypeStruct((b, s, BC_W), BF16),
        ],
        scratch_shapes=[pltpu.VMEM((cw // LANES, tm + 2 * HALO, LANES), F32),
                        pltpu.VMEM((cw // LANES, tm + 2 * HALO, LANES), F32),
                        pltpu.VMEM((cw // LANES, tm, LANES), F32),
                        pltpu.VMEM((cw // LANES, tm, LANES), F32)],
        compiler_params=_cparams(("parallel", "parallel")),
        name="xbc_conv_proj",
    )(x, x, x, w_xbc, conv_w, conv_b)


def _attn_kernel(qt_ref, k_ref, vt_ref, g_ref, o_ref):
    k = k_ref[0, 0]
    vt = vt_ref[0]
    vt_aug = jnp.concatenate([vt, jnp.ones((BF16_ROWS, vt.shape[1]), BF16)], axis=0)
    rep = N_Q_HEADS // N_KV_HEADS
    tq = qt_ref.shape[2]
    s_len = k.shape[0]
    kb = min(ATTN_KEY_BLOCK, s_len)
    nblk = s_len // kb

    qt = jnp.concatenate([qt_ref[0, r * HEAD_DIM:(r + 1) * HEAD_DIM, :] for r in range(rep)], axis=1)
    m = acc = None
    st_next = _dot(k[:kb], qt)
    for j in range(nblk):
        st = st_next
        if j + 1 < nblk:
            st_next = _dot(k[(j + 1) * kb:(j + 2) * kb], qt)
        bm = jnp.max(st, axis=0, keepdims=True)
        m_new = bm if m is None else jnp.maximum(m, bm)
        pt = jnp.exp2(st - m_new).astype(BF16)
        part = _dot(vt_aug[:, j * kb:(j + 1) * kb], pt)
        acc = part if acc is None else acc * jnp.exp2(m - m_new) + part
        m = m_new
    on = acc[:HEAD_DIM] / acc[HEAD_DIM:HEAD_DIM + 1]
    o = jnp.concatenate([on[:, r * tq:(r + 1) * tq].T for r in range(rep)], axis=1)
    o_ref[0] = (o * g_ref[0].astype(F32)).astype(BF16)


def _attn_call(qt, k, vt, gsil, tq):
    b, _, s = qt.shape
    gw = ATT_W // N_KV_HEADS
    return pl.pallas_call(
        _attn_kernel,
        grid=(b, N_KV_HEADS, s // tq),
        in_specs=[
            pl.BlockSpec((1, gw, tq), lambda i, g, j: (i, g, j)),
            pl.BlockSpec((1, 1, s, HEAD_DIM), lambda i, g, j: (i, g, 0, 0)),
            pl.BlockSpec((1, HEAD_DIM, s), lambda i, g, j: (i, g, 0)),
            pl.BlockSpec((1, tq, gw), lambda i, g, j: (i, j, g)),
        ],
        out_specs=pl.BlockSpec((1, tq, gw), lambda i, g, j: (i, j, g)),
        out_shape=jax.ShapeDtypeStruct((b, s, ATT_W), BF16),
        compiler_params=_cparams(("parallel", "parallel", "parallel")),
        name="attention",
    )(qt, k, vt, gsil)


def _pack_hilo(v, lane):
    hi = v.astype(BF16).astype(F32)
    lo = (v - hi).astype(BF16).astype(F32)
    return jnp.where(lane < LANES // 2, hi, pltpu.roll(lo, LANES // 2, 1)).astype(BF16)


def _chunk_decay_terms(dt, alog_ref, tril_ref):
    a = -jnp.exp(alog_ref[...])
    da = dt * a
    tril = tril_ref[...]
    hi, mid, lo = _split3(da)
    pre = _dot(tril, hi) + _dot(tril, mid) + _dot(tril, lo)
    tot = pre[CHUNK - 1:CHUNK, :]
    return da, pre, tot


def _ssd_bwd_kernel(x_ref, b_ref, dt_ref, alog_ref, tril_ref, eb_ref, hb_ref, h_scr):
    c = pl.program_id(1)

    @pl.when(c == 0)
    def _():
        h_scr[...] = jnp.zeros_like(h_scr)

    cps = x_ref.shape[1] // CHUNK
    terms = []
    for u in range(cps):
        rows = slice(u * CHUNK, (u + 1) * CHUNK)
        dt = dt_ref[0, rows, :]
        da, pre, tot = _chunk_decay_terms(dt, alog_ref, tril_ref)
        wgt = jnp.exp(pre - da) * dt
        dec = jnp.exp(jnp.broadcast_to(tot, (BF16_ROWS, LANES)))
        packed = _pack_hilo(jnp.concatenate([wgt, dec], axis=0),
                            lax.broadcasted_iota(jnp.int32, (CHUNK + BF16_ROWS, LANES), 1))
        ex = _dot(packed, eb_ref[...])
        xw = (x_ref[0, rows, :].astype(F32) * ex[:CHUNK]).astype(BF16)
        upd = [_dot_tn(b_ref[0, rows, g * D_STATE:(g + 1) * D_STATE], xw[:, g * GROUP_W:(g + 1) * GROUP_W])
               for g in range(N_SSM_GROUPS)]
        terms.append((ex[CHUNK:CHUNK + 1], upd))
    for u in reversed(range(cps)):
        decx, upd = terms[u]
        hb_ref[0, u] = h_scr[...].astype(BF16)
        for g in range(N_SSM_GROUPS):
            cols = slice(g * GROUP_W, (g + 1) * GROUP_W)
            h_scr[:, cols] = h_scr[:, cols] * decx[:, cols] + upd[g]


def _ssd_bwd_call(xs, bm, dt, alog, tril, eb):
    b, s, _ = xs.shape
    cps = min(SSD_BWD_CHUNKS_PER_STEP, s // CHUNK)
    assert s % (cps * CHUNK) == 0
    nc = s // (cps * CHUNK)
    rev = lambda i, c: (i, nc - 1 - c, 0)
    return pl.pallas_call(
        _ssd_bwd_kernel,
        grid=(b, nc),
        in_specs=[
            pl.BlockSpec((1, cps * CHUNK, D_INNER), rev),
            pl.BlockSpec((1, cps * CHUNK, BC_W), rev),
            pl.BlockSpec((1, cps * CHUNK, LANES), rev),
            pl.BlockSpec(alog.shape, lambda i, c: (0, 0)),
            pl.BlockSpec(tril.shape, lambda i, c: (0, 0)),
            pl.BlockSpec(eb.shape, lambda i, c: (0, 0)),
        ],
        out_specs=pl.BlockSpec((1, cps, D_STATE, D_INNER), lambda i, c: (i, nc - 1 - c, 0, 0)),
        out_shape=jax.ShapeDtypeStruct((b, s // CHUNK, D_STATE, D_INNER), BF16),
        scratch_shapes=[pltpu.VMEM((D_STATE, D_INNER), F32)],
        compiler_params=_cparams(("parallel", "arbitrary")),
        name="ssd_bwd_states",
    )(xs, bm, dt, alog, tril, eb)


def _ssd_fwd_kernel(x_ref, b_ref, c_ref, dt_ref, z_ref, hb_ref, alog_ref, dskip_ref, nw_ref,
                    tril_ref, ef_ref, eb_ref, sel_ref, y_ref, h_scr):
    c = pl.program_id(1)

    @pl.when(c == 0)
    def _():
        h_scr[...] = jnp.zeros_like(h_scr)

    half = N_SSM_HEADS
    lane = lax.broadcasted_iota(jnp.int32, (CHUNK, LANES), 1)
    row = lax.broadcasted_iota(jnp.int32, (CHUNK, CHUNK), 0)
    col = lax.broadcasted_iota(jnp.int32, (CHUNK, CHUNK), 1)
    lower = col <= row
    eye = col == row
    first_head = lane < SSM_HEAD_DIM
    grp = (lane - half) >> 3
    ef = ef_ref[...]
    eb = eb_ref[...]

    def state_free_part(u):
        rows = slice(u * CHUNK, (u + 1) * CHUNK)
        dt = dt_ref[0, rows, :]
        da, pre, tot = _chunk_decay_terms(dt, alog_ref, tril_ref)
        acs = jnp.where(lane < half, pre, tot - pre + da)
        acsp_t = (acs - jnp.log(dt)).T
        cols = _dot(jnp.concatenate(_split2(acs), axis=1), sel_ref[...])
        xb = x_ref[0, rows, :]
        hb = hb_ref[0, u]
        y_diag = []
        y_off_b = []
        cb_diag = []
        for g in range(N_SSM_GROUPS):
            bg = b_ref[0, rows, g * D_STATE:(g + 1) * D_STATE]
            cg = c_ref[0, rows, g * D_STATE:(g + 1) * D_STATE]
            cb = _dot_nt(cg, bg)
            cb_diag.append(jnp.sum(jnp.where(eye, cb, 0.0), axis=-1, keepdims=True))
            for k in range(HEADS_PER_GROUP // 2):
                pair = g * (HEADS_PER_GROUP // 2) + k
                col_b = cols[:, pair * 2 * CHUNK:(pair + 1) * 2 * CHUNK]
                ms = []
                for t in range(2):
                    h = 2 * pair + t
                    arg = jnp.where(lower,
                                    acs[:, h:h + 1] - acsp_t[h:h + 1, :],
                                    col_b[:, t * CHUNK:(t + 1) * CHUNK] - acsp_t[half + h:half + h + 1, :])
                    ms.append((cb * jnp.exp(arg)).astype(BF16))
                xp = xb[:, pair * LANES:(pair + 1) * LANES]
                zero = jnp.zeros_like(xp)
                rhs = jnp.concatenate([jnp.where(first_head, xp, zero), jnp.where(first_head, zero, xp)], axis=0)
                y_diag.append(_dot(jnp.concatenate(ms, axis=1), rhs))
            y_off_b.append(_dot(cg, hb[:, g * GROUP_W:(g + 1) * GROUP_W]))
        y_diag = jnp.concatenate(y_diag, axis=1)
        y_off_b = jnp.concatenate(y_off_b, axis=1)

        diag_h = jnp.zeros((CHUNK, LANES), F32)
        for g in range(N_SSM_GROUPS):
            diag_h = jnp.where(grp == g, cb_diag[g], diag_h)
        coef = dskip_ref[...] + diag_h * dt
        scale = jnp.exp(acs)
        wgt = jnp.exp(tot - pre) * dt
        dec = jnp.exp(jnp.broadcast_to(tot, (BF16_ROWS, LANES)))
        pk_scale = _pack_hilo(scale, lane)
        scale_f = _dot(pk_scale, ef)
        scale_b = _dot(pk_scale, eb)
        coef_x = _dot(_pack_hilo(coef, lane), eb)
        pk_wd = _pack_hilo(jnp.concatenate([wgt, dec], axis=0),
                           lax.broadcasted_iota(jnp.int32, (CHUNK + BF16_ROWS, LANES), 1))
        wd_x = _dot(pk_wd, ef)
        xf = xb.astype(F32)
        y_part = y_diag + y_off_b * scale_b + coef_x * xf
        xw = (xf * wd_x[:CHUNK]).astype(BF16)
        upd = [_dot_tn(b_ref[0, rows, g * D_STATE:(g + 1) * D_STATE], xw[:, g * GROUP_W:(g + 1) * GROUP_W])
               for g in range(N_SSM_GROUPS)]
        return y_part, scale_f, wd_x[CHUNK:CHUNK + 1], upd

    cps = x_ref.shape[1] // CHUNK
    parts = [state_free_part(u) for u in range(cps)]
    for u in range(cps):
        rows = slice(u * CHUNK, (u + 1) * CHUNK)
        y_part, scale_f, decx, upd = parts[u]
        for g in range(N_SSM_GROUPS):
            cols = slice(g * GROUP_W, (g + 1) * GROUP_W)
            cg = c_ref[0, rows, g * D_STATE:(g + 1) * D_STATE]
            y_off_f = _dot(cg, h_scr[:, cols].astype(BF16))
            yg = (y_part[:, cols] + y_off_f * scale_f[:, cols]) * z_ref[0, rows, cols].astype(F32)
            ms = jnp.mean(yg * yg, axis=-1, keepdims=True)
            y_ref[0, rows, cols] = (yg * lax.rsqrt(ms + RMS_EPS) * nw_ref[:, cols]).astype(BF16)
            h_scr[:, cols] = h_scr[:, cols] * decx[:, cols] + upd[g]


def _ssd_fwd_call(xs, bm, cm, dt, zsil, hb, alog, dskip, nw, tril, ef, eb, sel):
    b, s, _ = xs.shape
    cps = min(SSD_FWD_CHUNKS_PER_STEP, s // CHUNK)
    assert s % (cps * CHUNK) == 0
    nc = s // (cps * CHUNK)
    fwd = lambda i, c: (i, c, 0)
    const = lambda i, c: (0, 0)
    return pl.pallas_call(
        _ssd_fwd_kernel,
        grid=(b, nc),
        in_specs=[
            pl.BlockSpec((1, cps * CHUNK, D_INNER), fwd),
            pl.BlockSpec((1, cps * CHUNK, BC_W), fwd),
            pl.BlockSpec((1, cps * CHUNK, BC_W), fwd),
            pl.BlockSpec((1, cps * CHUNK, LANES), fwd),
            pl.BlockSpec((1, cps * CHUNK, D_INNER), fwd),
            pl.BlockSpec((1, cps, D_STATE, D_INNER), lambda i, c: (i, c, 0, 0)),
            pl.BlockSpec(alog.shape, const),
            pl.BlockSpec(dskip.shape, const),
            pl.BlockSpec(nw.shape, const),
            pl.BlockSpec(tril.shape, const),
            pl.BlockSpec(ef.shape, const),
            pl.BlockSpec(eb.shape, const),
            pl.BlockSpec(sel.shape, const),
        ],
        out_specs=pl.BlockSpec((1, cps * CHUNK, D_INNER), fwd),
        out_shape=jax.ShapeDtypeStruct((b, s, D_INNER), BF16),
        scratch_shapes=[pltpu.VMEM((D_STATE, D_INNER), F32)],
        compiler_params=_cparams(("parallel", "arbitrary")),
        name="ssd_fwd",
    )(xs, bm, cm, dt, zsil, hb, alog, dskip, nw, tril, ef, eb, sel)


def _out_kernel(x_ref, att_ref, y_ref, gate_ref, wa_ref, ws_ref, wo_ref, lg_ref, lb_ref, o_ref):
    att = _dot(att_ref[0], wa_ref[...])
    ssm = _dot(y_ref[0], ws_ref[...])
    ga = gate_ref[0, :, :D_MODEL].astype(F32)
    gs = gate_ref[0, :, D_MODEL:].astype(F32)
    mixed = ga * att + gs * ssm
    out = _dot(mixed.astype(BF16), wo_ref[...])
    r = ALPHA * x_ref[0] + out
    mu = jnp.mean(r, axis=-1, keepdims=True)
    d = r - mu
    var = jnp.mean(d * d, axis=-1, keepdims=True)
    o_ref[0] = d * lax.rsqrt(var + LN_EPS) * lg_ref[...] + lb_ref[...]


def _out_call(x, att, y, gates, wa, ws, wo, lg, lb, tm):
    b, s, _ = x.shape
    tok = lambda i, j: (i, j, 0)
    const = lambda i, j: (0, 0)
    return pl.pallas_call(
        _out_kernel,
        grid=(b, s // tm),
        in_specs=[
            pl.BlockSpec((1, tm, D_MODEL), tok),
            pl.BlockSpec((1, tm, ATT_W), tok),
            pl.BlockSpec((1, tm, D_INNER), tok),
            pl.BlockSpec((1, tm, 2 * D_MODEL), tok),
            pl.BlockSpec(wa.shape, const),
            pl.BlockSpec(ws.shape, const),
            pl.BlockSpec(wo.shape, const),
            pl.BlockSpec(lg.shape, const),
            pl.BlockSpec(lb.shape, const),
        ],
        out_specs=pl.BlockSpec((1, tm, D_MODEL), tok),
        out_shape=jax.ShapeDtypeStruct((b, s, D_MODEL), F32),
        compiler_params=_cparams(("parallel", "parallel")),
        name="merge_out_ln",
    )(x, att, y, gates, wa, ws, wo, lg, lb)


def _rope_tables(seq_len):
    rows = seq_len // GRID_W
    row = jnp.repeat(jnp.arange(rows, dtype=F32), GRID_W)
    col = jnp.tile(jnp.arange(GRID_W, dtype=F32), rows)
    freqs = ROPE_THETA ** (-jnp.arange(0, AXIS_DIM, 2, dtype=F32) / AXIS_DIM)
    ang = jnp.concatenate([row[:, None] * freqs, col[:, None] * freqs], axis=-1)
    return jnp.cos(ang).T, jnp.sin(ang).T


def _expand_matrix(lane_offset):
    r = np.arange(LANES)[:, None] % (LANES // 2)
    head = np.arange(D_INNER)[None, :] // SSM_HEAD_DIM
    return jnp.asarray((r == head + lane_offset).astype(np.float32), dtype=BF16)


def _select_matrices():
    lane = np.arange(2 * LANES)[:, None] % LANES
    second = (np.arange(2 * CHUNK)[None, :] >= CHUNK).astype(np.int64)
    mats = [(lane == N_SSM_HEADS + 2 * p + second) for p in range(N_SSM_HEADS // 2)]
    return jnp.asarray(np.concatenate(mats, axis=1).astype(np.float32), dtype=BF16)


def _layer(x, p, tables):
    b, s, _ = x.shape
    tm = min(MAX_TOKEN_TILE, s)
    tq = min(MAX_QUERY_TILE, s)
    cos_t, sin_t = tables
    qt, k, vt = _qkv_call(x, p["wt_qkv"], cos_t[:, :s], sin_t[:, :s], p["qw"][:, :tm], p["kw"][:, :tm], tm)
    gsil, zsil, gates, dt = _gates_call(x, p["w_gates"], p["b_gate"], p["b_dt"], min(GATES_TOKEN_TILE, s))
    xs, bm, cm = _xbc_call(x, p["w_xbc"], p["conv_w"], p["conv_b"], tm)
    att = _attn_call(qt, k, vt, gsil, tq)
    hb = _ssd_bwd_call(xs, bm, dt, p["alog"], p["tril"], p["eb"])
    y = _ssd_fwd_call(xs, bm, cm, dt, zsil, hb, p["alog"], p["dskip"], p["nw"], p["tril"], p["ef"], p["eb"],
                      p["sel"])
    return _out_call(x, att, y, gates, p["wa"], p["ws"], p["wo"], p["lg"], p["lb"], tm)


def _prepare(w_in, b_gate, q_norm_w, k_norm_w, conv_w, conv_b, dt_bias_fwd, dt_bias_bwd,
             a_log_fwd, a_log_bwd, d_skip, ssm_norm_w, w_att_proj, w_ssm_proj, w_out, ln_g, ln_b):
    wq, wk, wv, wg, wz, wxbc, wdt, wgate = jnp.split(w_in, SPLIT_POINTS, axis=-1)
    perm = np.concatenate([np.arange(0, HEAD_DIM, 2), np.arange(1, HEAD_DIM, 2)])
    qperm = (np.arange(N_Q_HEADS)[:, None] * HEAD_DIM + perm[None, :]).reshape(-1)
    kperm = (np.arange(N_KV_HEADS)[:, None] * HEAD_DIM + perm[None, :]).reshape(-1)
    pad = LANES - 2 * N_SSM_HEADS
    zeros_h = jnp.zeros((N_SSM_HEADS,), F32)
    zeros_p = jnp.zeros((pad,), F32)
    return {
        "wt_qkv": jnp.concatenate([wq[:, qperm], wk[:, kperm], wv], axis=1).T.astype(BF16),
        "qw": jnp.broadcast_to(q_norm_w[perm][:, None], (HEAD_DIM, MAX_TOKEN_TILE)),
        "kw": jnp.broadcast_to(k_norm_w[perm][:, None], (HEAD_DIM, MAX_TOKEN_TILE)),
        "w_gates": jnp.concatenate([wg, wz, wgate, wdt, jnp.zeros((D_MODEL, pad), F32)], axis=1).astype(BF16),
        "w_xbc": wxbc.astype(BF16),
        "conv_w": conv_w,
        "conv_b": conv_b[None, :],
        "b_gate": b_gate[None, :],
        "b_dt": jnp.concatenate([dt_bias_fwd, dt_bias_bwd, zeros_p])[None, :],
        "alog": jnp.concatenate([a_log_fwd, a_log_bwd, zeros_p])[None, :],
        "dskip": jnp.concatenate([zeros_h, d_skip, zeros_p])[None, :],
        "nw": ssm_norm_w[None, :],
        "tril": jnp.asarray(np.tril(np.ones((CHUNK, CHUNK), np.float32)), dtype=BF16),
        "ef": _expand_matrix(0),
        "eb": _expand_matrix(N_SSM_HEADS),
        "sel": _select_matrices(),
        "wa": w_att_proj.astype(BF16),
        "ws": w_ssm_proj.astype(BF16),
        "wo": w_out.astype(BF16),
        "lg": ln_g[None, :],
        "lb": ln_b[None, :],
    }


def kernel(x_prompt, x_sample, w_in, b_gate, q_norm_w, k_norm_w, conv_w, conv_b, dt_bias_fwd,
           dt_bias_bwd, a_log_fwd, a_log_bwd, d_skip, ssm_norm_w, w_att_proj, w_ssm_proj, w_out,
           ln_g, ln_b):
    xp, xs = x_prompt, x_sample
    tables = _rope_tables(max(xp.shape[1], xs.shape[1]))
    for l in range(w_in.shape[0]):
        p = _prepare(w_in[l], b_gate[l], q_norm_w[l], k_norm_w[l], conv_w[l], conv_b[l],
                     dt_bias_fwd[l], dt_bias_bwd[l], a_log_fwd[l], a_log_bwd[l], d_skip[l],
                     ssm_norm_w[l], w_att_proj[l], w_ssm_proj[l], w_out[l], ln_g[l], ln_b[l])
        xp = _layer(xp, p, tables)
        xs = _layer(xs, p, tables)
    return (xp, xs)
```

```python
import numpy as np
import jax
import jax.numpy as jnp
from jax import lax
from jax.experimental import pallas as pl
from jax.experimental.pallas import tpu as pltpu

F32 = jnp.float32
BF16 = jnp.bfloat16

D_MODEL = 1024
GRID_W = 64
N_Q_HEADS = 16
N_KV_HEADS = 4
HEAD_DIM = 64
ATT_W = N_Q_HEADS * HEAD_DIM
KV_W = N_KV_HEADS * HEAD_DIM
AXIS_DIM = HEAD_DIM // 2
ROPE_THETA = 10000.0
D_INNER = 2 * D_MODEL
SSM_HEAD_DIM = 64
N_SSM_HEADS = D_INNER // SSM_HEAD_DIM
N_SSM_GROUPS = 4
D_STATE = 128
CONV_W = 5
CHUNK = 128
BC_W = N_SSM_GROUPS * D_STATE
CONV_CH = D_INNER + 2 * BC_W
RMS_EPS = 1e-6
LN_EPS = 1e-5
DEPTH = 1
ALPHA = (2 * DEPTH) ** 0.25
SPLITS = (ATT_W, KV_W, KV_W, ATT_W, D_INNER, CONV_CH, 2 * N_SSM_HEADS, 2 * D_MODEL)
SPLIT_POINTS = tuple(int(v) for v in np.cumsum(SPLITS)[:-1])

LANES = 128
HALO = 8
BF16_ROWS = 16
CONV_PHASES = 4
CONV_CHUNK = 512
SSD_BWD_CHUNKS_PER_STEP = 8
SSD_FWD_CHUNKS_PER_STEP = 4
GROUP_W = D_INNER // N_SSM_GROUPS
HEADS_PER_GROUP = N_SSM_HEADS // N_SSM_GROUPS
VMEM_LIMIT = 48 * 1024 * 1024
MAX_TOKEN_TILE = 512
GATES_TOKEN_TILE = 256
XBC_TOKEN_TILE = 1024
MAX_QUERY_TILE = 1024
ATTN_KEY_BLOCK = 256
Q_SCALE = float(np.log2(np.e)) * HEAD_DIM ** -0.5


def _cparams(sem):
    return pltpu.CompilerParams(dimension_semantics=sem, vmem_limit_bytes=VMEM_LIMIT)


def _dot(a, b):
    return jnp.dot(a, b, preferred_element_type=F32)


def _dot_nt(a, b):
    return lax.dot_general(a, b, (((1,), (1,)), ((), ())), preferred_element_type=F32)


def _dot_tn(a, b):
    return lax.dot_general(a, b, (((0,), (0,)), ((), ())), preferred_element_type=F32)


def _silu(x):
    return x / (1.0 + jnp.exp(-x))


def _split2(v):
    hi = v.astype(BF16)
    lo = (v - hi.astype(F32)).astype(BF16)
    return hi, lo


def _split3(v):
    hi = v.astype(BF16)
    r = v - hi.astype(F32)
    mid = r.astype(BF16)
    lo = (r - mid.astype(F32)).astype(BF16)
    return hi, mid, lo


def _qkv_kernel(x_ref, wt_ref, cos_ref, sin_ref, qw_ref, kw_ref, qt_ref, k_ref, vt_ref):
    xb = x_ref[0].astype(BF16)
    cosv = cos_ref[...]
    sinv = sin_ref[...]
    half = HEAD_DIM // 2

    def norm_rope(acc, w_ref, scale):
        x0, x1 = acc[:half], acc[half:]
        ms = (jnp.sum(x0 * x0, axis=0, keepdims=True) + jnp.sum(x1 * x1, axis=0, keepdims=True)) * (1.0 / HEAD_DIM)
        inv = lax.rsqrt(ms + RMS_EPS)
        a0 = x0 * inv * w_ref[:half]
        a1 = x1 * inv * w_ref[half:]
        return (a0 * cosv - a1 * sinv) * scale, (a0 * sinv + a1 * cosv) * scale

    rows = 4 * HEAD_DIM
    for c in range(ATT_W // rows):
        acc = _dot_nt(wt_ref[c * rows:(c + 1) * rows, :], xb)
        for h in range(rows // HEAD_DIM):
            o0, o1 = norm_rope(acc[h * HEAD_DIM:(h + 1) * HEAD_DIM], qw_ref, Q_SCALE)
            r0 = c * rows + h * HEAD_DIM
            qt_ref[0, r0:r0 + half, :] = o0.astype(BF16)
            qt_ref[0, r0 + half:r0 + HEAD_DIM, :] = o1.astype(BF16)
    acc = _dot_nt(wt_ref[ATT_W:ATT_W + KV_W, :], xb)
    for g in range(N_KV_HEADS):
        o0, o1 = norm_rope(acc[g * HEAD_DIM:(g + 1) * HEAD_DIM], kw_ref, 1.0)
        k_ref[0, g] = jnp.concatenate([o0, o1], axis=0).T.astype(BF16)
    vt_ref[0] = _dot_nt(wt_ref[ATT_W + KV_W:ATT_W + 2 * KV_W, :], xb).astype(BF16)


def _qkv_call(x, wt_qkv, cos_t, sin_t, qw, kw, tm):
    b, s, _ = x.shape
    half = HEAD_DIM // 2
    return pl.pallas_call(
        _qkv_kernel,
        grid=(b, s // tm),
        in_specs=[
            pl.BlockSpec((1, tm, D_MODEL), lambda i, j: (i, j, 0)),
            pl.BlockSpec(wt_qkv.shape, lambda i, j: (0, 0)),
            pl.BlockSpec((half, tm), lambda i, j: (0, j)),
            pl.BlockSpec((half, tm), lambda i, j: (0, j)),
            pl.BlockSpec(qw.shape, lambda i, j: (0, 0)),
            pl.BlockSpec(kw.shape, lambda i, j: (0, 0)),
        ],
        out_specs=[
            pl.BlockSpec((1, ATT_W, tm), lambda i, j: (i, 0, j)),
            pl.BlockSpec((1, N_KV_HEADS, tm, HEAD_DIM), lambda i, j: (i, 0, j, 0)),
            pl.BlockSpec((1, KV_W, tm), lambda i, j: (i, 0, j)),
        ],
        out_shape=[
            jax.ShapeDtypeStruct((b, ATT_W, s), BF16),
            jax.ShapeDtypeStruct((b, N_KV_HEADS, s, HEAD_DIM), BF16),
            jax.ShapeDtypeStruct((b, KV_W, s), BF16),
        ],
        compiler_params=_cparams(("parallel", "parallel")),
        name="qkv_proj",
    )(x, wt_qkv, cos_t, sin_t, qw, kw)


def _gates_kernel(x_ref, w_ref, bg_ref, bdt_ref, g_ref, z_ref, gate_ref, dt_ref):
    xb = x_ref[0].astype(BF16)
    cw = 4 * LANES
    off = 0
    for c in range(ATT_W // cw):
        acc = _dot(xb, w_ref[:, off + c * cw:off + (c + 1) * cw])
        g_ref[0, :, c * cw:(c + 1) * cw] = _silu(acc).astype(BF16)
    off += ATT_W
    for c in range(D_INNER // cw):
        acc = _dot(xb, w_ref[:, off + c * cw:off + (c + 1) * cw])
        z_ref[0, :, c * cw:(c + 1) * cw] = _silu(acc).astype(BF16)
    off += D_INNER
    for c in range(2 * D_MODEL // cw):
        acc = _dot(xb, w_ref[:, off + c * cw:off + (c + 1) * cw]) + bg_ref[:, c * cw:(c + 1) * cw]
        gate_ref[0, :, c * cw:(c + 1) * cw] = (1.0 / (1.0 + jnp.exp(-acc))).astype(BF16)
    off += 2 * D_MODEL
    v = _dot(xb, w_ref[:, off:off + LANES]) + bdt_ref[...]
    dt_ref[0] = jnp.maximum(v, 0.0) + jnp.log1p(jnp.exp(-jnp.abs(v)))


def _gates_call(x, w_gates, b_gate, b_dt, tm):
    b, s, _ = x.shape
    tok = lambda i, j: (i, j, 0)
    const = lambda i, j: (0, 0)
    return pl.pallas_call(
        _gates_kernel,
        grid=(b, s // tm),
        in_specs=[
            pl.BlockSpec((1, tm, D_MODEL), tok),
            pl.BlockSpec(w_gates.shape, const),
            pl.BlockSpec(b_gate.shape, const),
            pl.BlockSpec(b_dt.shape, const),
        ],
        out_specs=[
            pl.BlockSpec((1, tm, ATT_W), tok),
            pl.BlockSpec((1, tm, D_INNER), tok),
            pl.BlockSpec((1, tm, 2 * D_MODEL), tok),
            pl.BlockSpec((1, tm, LANES), tok),
        ],
        out_shape=[
            jax.ShapeDtypeStruct((b, s, ATT_W), BF16),
            jax.ShapeDtypeStruct((b, s, D_INNER), BF16),
            jax.ShapeDtypeStruct((b, s, 2 * D_MODEL), BF16),
            jax.ShapeDtypeStruct((b, s, LANES), F32),
        ],
        compiler_params=_cparams(("parallel", "parallel")),
        name="gates_proj",
    )(x, w_gates, b_gate, b_dt)


def _xbc_kernel(x_ref, xp_ref, xn_ref, w_ref, cw_ref, cb_ref, xs_ref, b_ref, c_ref,
                acc_a, acc_b, y_a, y_b):
    tm = x_ref.shape[1]
    accs = (acc_a, acc_b)
    ys = (y_a, y_b)
    j = pl.program_id(1)
    nj = pl.num_programs(1)
    prev = jnp.where(j > 0, xp_ref[0], 0.0)
    nxt = jnp.where(j < nj - 1, xn_ref[0], 0.0)
    xall = jnp.concatenate([prev, x_ref[0], nxt], axis=0).astype(BF16)
    cw = CONV_CHUNK
    slabs = cw // LANES
    outs = ([(xs_ref, c * cw) for c in range(D_INNER // cw)] + [(b_ref, c * cw) for c in range(BC_W // cw)]
            + [(c_ref, c * cw) for c in range(BC_W // cw)])

    def project(c):
        acc = _dot(xall, w_ref[:, c * cw:(c + 1) * cw])
        for s in range(slabs):
            accs[c % 2][s] = acc[:, s * LANES:(s + 1) * LANES]

    project(0)
    for c, (o_ref, off) in enumerate(outs):
        if c + 1 < len(outs):
            project(c + 1)
        for s in range(slabs):
            col = c * cw + s * LANES
            for p in range(CONV_PHASES):
                y = cb_ref[:, col:col + LANES]
                for k in range(CONV_W):
                    r0 = HALO - CONV_W // 2 + k + p
                    tap = accs[c % 2][s, pl.ds(r0, tm // CONV_PHASES, stride=CONV_PHASES), :]
                    y = y + tap * cw_ref[k:k + 1, col:col + LANES]
                ys[c % 2][s, pl.ds(p, tm // CONV_PHASES, stride=CONV_PHASES), :] = _silu(y)
        o_ref[0, :, off:off + cw] = jnp.concatenate([ys[c % 2][s] for s in range(slabs)], axis=1).astype(BF16)


def _xbc_call(x, w_xbc, conv_w, conv_b, tm):
    b, s, _ = x.shape
    hb = tm // HALO
    last = s // HALO - 1
    cw = CONV_CHUNK
    return pl.pallas_call(
        _xbc_kernel,
        grid=(b, s // tm),
        in_specs=[
            pl.BlockSpec((1, tm, D_MODEL), lambda i, j: (i, j, 0)),
            pl.BlockSpec((1, HALO, D_MODEL), lambda i, j: (i, jnp.maximum(j * hb - 1, 0), 0)),
            pl.BlockSpec((1, HALO, D_MODEL), lambda i, j: (i, jnp.minimum((j + 1) * hb, last), 0)),
            pl.BlockSpec(w_xbc.shape, lambda i, j: (0, 0)),
            pl.BlockSpec(conv_w.shape, lambda i, j: (0, 0)),
            pl.BlockSpec(conv_b.shape, lambda i, j: (0, 0)),
        ],
        out_specs=[
            pl.BlockSpec((1, tm, D_INNER), lambda i, j: (i, j, 0)),
            pl.BlockSpec((1, tm, BC_W), lambda i, j: (i, j, 0)),
            pl.BlockSpec((1, tm, BC_W), lambda i, j: (i, j, 0)),
        ],
        out_shape=[
            jax.ShapeDtypeStruct((b, s, D_INNER), BF16),
            jax.ShapeDtypeStruct((b, s, BC_W), BF16),
            jax.ShapeDtypeStruct((b, s, BC_W), BF16),
        ],
        scratch_shapes=[pltpu.VMEM((cw // LANES, tm + 2 * HALO, LANES), F32),
                        pltpu.VMEM((cw // LANES, tm + 2 * HALO, LANES), F32),
                        pltpu.VMEM((cw // LANES, tm, LANES), F32),
                        pltpu.VMEM((cw // LANES, tm, LANES), F32)],
        compiler_params=_cparams(("parallel", "parallel")),
        name="xbc_conv_proj",
    )(x, x, x, w_xbc, conv_w, conv_b)


def _attn_kernel(qt_ref, k_ref, vt_ref, g_ref, o_ref):
    k = k_ref[0, 0]
    vt = vt_ref[0]
    vt_aug = jnp.concatenate([vt, jnp.ones((BF16_ROWS, vt.shape[1]), BF16)], axis=0)
    rep = N_Q_HEADS // N_KV_HEADS
    tq = qt_ref.shape[2]
    s_len = k.shape[0]
    kb = min(ATTN_KEY_BLOCK, s_len)
    nblk = s_len // kb

    qt = jnp.concatenate([qt_ref[0, r * HEAD_DIM:(r + 1) * HEAD_DIM, :] for r in range(rep)], axis=1)
    m = acc = None
    st_next = _dot(k[:kb], qt)
    for j in range(nblk):
        st = st_next
        if j + 1 < nblk:
            st_next = _dot(k[(j + 1) * kb:(j + 2) * kb], qt)
        bm = jnp.max(st, axis=0, keepdims=True)
        m_new = bm if m is None else jnp.maximum(m, bm)
        pt = jnp.exp2(st - m_new).astype(BF16)
        part = _dot(vt_aug[:, j * kb:(j + 1) * kb], pt)
        acc = part if acc is None else acc * jnp.exp2(m - m_new) + part
        m = m_new
    on = acc[:HEAD_DIM] / acc[HEAD_DIM:HEAD_DIM + 1]
    o = jnp.concatenate([on[:, r * tq:(r + 1) * tq].T for r in range(rep)], axis=1)
    o_ref[0] = (o * g_ref[0].astype(F32)).astype(BF16)


def _attn_call(qt, k, vt, gsil, tq):
    b, _, s = qt.shape
    gw = ATT_W // N_KV_HEADS
    return pl.pallas_call(
        _attn_kernel,
        grid=(b, N_KV_HEADS, s // tq),
        in_specs=[
            pl.BlockSpec((1, gw, tq), lambda i, g, j: (i, g, j)),
            pl.BlockSpec((1, 1, s, HEAD_DIM), lambda i, g, j: (i, g, 0, 0)),
            pl.BlockSpec((1, HEAD_DIM, s), lambda i, g, j: (i, g, 0)),
            pl.BlockSpec((1, tq, gw), lambda i, g, j: (i, j, g)),
        ],
        out_specs=pl.BlockSpec((1, tq, gw), lambda i, g, j: (i, j, g)),
        out_shape=jax.ShapeDtypeStruct((b, s, ATT_W), BF16),
        compiler_params=_cparams(("parallel", "parallel", "parallel")),
        name="attention",
    )(qt, k, vt, gsil)


def _pack_hilo(v, lane):
    hi = v.astype(BF16).astype(F32)
    lo = (v - hi).astype(BF16).astype(F32)
    return jnp.where(lane < LANES // 2, hi, pltpu.roll(lo, LANES // 2, 1)).astype(BF16)


def _chunk_decay_terms(dt, alog_ref, tril_ref):
    a = -jnp.exp(alog_ref[...])
    da = dt * a
    tril = tril_ref[...]
    hi, mid, lo = _split3(da)
    pre = _dot(tril, hi) + _dot(tril, mid) + _dot(tril, lo)
    tot = pre[CHUNK - 1:CHUNK, :]
    return da, pre, tot


def _ssd_bwd_kernel(x_ref, b_ref, dt_ref, alog_ref, tril_ref, eb_ref, hb_ref, h_scr):
    c = pl.program_id(1)

    @pl.when(c == 0)
    def _():
        h_scr[...] = jnp.zeros_like(h_scr)

    cps = x_ref.shape[1] // CHUNK
    terms = []
    for u in range(cps):
        rows = slice(u * CHUNK, (u + 1) * CHUNK)
        dt = dt_ref[0, rows, :]
        da, pre, tot = _chunk_decay_terms(dt, alog_ref, tril_ref)
        wgt = jnp.exp(pre - da) * dt
        dec = jnp.exp(jnp.broadcast_to(tot, (BF16_ROWS, LANES)))
        packed = _pack_hilo(jnp.concatenate([wgt, dec], axis=0),
                            lax.broadcasted_iota(jnp.int32, (CHUNK + BF16_ROWS, LANES), 1))
        ex = _dot(packed, eb_ref[...])
        xw = (x_ref[0, rows, :].astype(F32) * ex[:CHUNK]).astype(BF16)
        upd = [_dot_tn(b_ref[0, rows, g * D_STATE:(g + 1) * D_STATE], xw[:, g * GROUP_W:(g + 1) * GROUP_W])
               for g in range(N_SSM_GROUPS)]
        terms.append((ex[CHUNK:CHUNK + 1], upd))
    for u in reversed(range(cps)):
        decx, upd = terms[u]
        hb_ref[0, u] = h_scr[...].astype(BF16)
        for g in range(N_SSM_GROUPS):
            cols = slice(g * GROUP_W, (g + 1) * GROUP_W)
            h_scr[:, cols] = h_scr[:, cols] * decx[:, cols] + upd[g]


def _ssd_bwd_call(xs, bm, dt, alog, tril, eb):
    b, s, _ = xs.shape
    cps = min(SSD_BWD_CHUNKS_PER_STEP, s // CHUNK)
    assert s % (cps * CHUNK) == 0
    nc = s // (cps * CHUNK)
    rev = lambda i, c: (i, nc - 1 - c, 0)
    return pl.pallas_call(
        _ssd_bwd_kernel,
        grid=(b, nc),
        in_specs=[
            pl.BlockSpec((1, cps * CHUNK, D_INNER), rev),
            pl.BlockSpec((1, cps * CHUNK, BC_W), rev),
            pl.BlockSpec((1, cps * CHUNK, LANES), rev),
            pl.BlockSpec(alog.shape, lambda i, c: (0, 0)),
            pl.BlockSpec(tril.shape, lambda i, c: (0, 0)),
            pl.BlockSpec(eb.shape, lambda i, c: (0, 0)),
        ],
        out_specs=pl.BlockSpec((1, cps, D_STATE, D_INNER), lambda i, c: (i, nc - 1 - c, 0, 0)),
        out_shape=jax.ShapeDtypeStruct((b, s // CHUNK, D_STATE, D_INNER), BF16),
        scratch_shapes=[pltpu.VMEM((D_STATE, D_INNER), F32)],
        compiler_params=_cparams(("parallel", "arbitrary")),
        name="ssd_bwd_states",
    )(xs, bm, dt, alog, tril, eb)


def _ssd_fwd_kernel(x_ref, b_ref, c_ref, dt_ref, z_ref, hb_ref, alog_ref, dskip_ref, nw_ref,
                    tril_ref, ef_ref, eb_ref, sel_ref, y_ref, h_scr):
    c = pl.program_id(1)

    @pl.when(c == 0)
    def _():
        h_scr[...] = jnp.zeros_like(h_scr)

    half = N_SSM_HEADS
    lane = lax.broadcasted_iota(jnp.int32, (CHUNK, LANES), 1)
    row = lax.broadcasted_iota(jnp.int32, (CHUNK, CHUNK), 0)
    col = lax.broadcasted_iota(jnp.int32, (CHUNK, CHUNK), 1)
    lower = col <= row
    eye = col == row
    first_head = lane < SSM_HEAD_DIM
    grp = (lane - half) >> 3
    ef = ef_ref[...]
    eb = eb_ref[...]

    def state_free_part(u):
        rows = slice(u * CHUNK, (u + 1) * CHUNK)
        dt = dt_ref[0, rows, :]
        da, pre, tot = _chunk_decay_terms(dt, alog_ref, tril_ref)
        acs = jnp.where(lane < half, pre, tot - pre + da)
        acsp_t = (acs - jnp.log(dt)).T
        cols = _dot(jnp.concatenate(_split2(acs), axis=1), sel_ref[...])
        xb = x_ref[0, rows, :]
        hb = hb_ref[0, u]
        y_diag = []
        y_off_b = []
        cb_diag = []
        for g in range(N_SSM_GROUPS):
            bg = b_ref[0, rows, g * D_STATE:(g + 1) * D_STATE]
            cg = c_ref[0, rows, g * D_STATE:(g + 1) * D_STATE]
            cb = _dot_nt(cg, bg)
            cb_diag.append(jnp.sum(jnp.where(eye, cb, 0.0), axis=-1, keepdims=True))
            for k in range(HEADS_PER_GROUP // 2):
                pair = g * (HEADS_PER_GROUP // 2) + k
                col_b = cols[:, pair * 2 * CHUNK:(pair + 1) * 2 * CHUNK]
                ms = []
                for t in range(2):
                    h = 2 * pair + t
                    arg = jnp.where(lower,
                                    acs[:, h:h + 1] - acsp_t[h:h + 1, :],
                                    col_b[:, t * CHUNK:(t + 1) * CHUNK] - acsp_t[half + h:half + h + 1, :])
                    ms.append((cb * jnp.exp(arg)).astype(BF16))
                xp = xb[:, pair * LANES:(pair + 1) * LANES]
                zero = jnp.zeros_like(xp)
                rhs = jnp.concatenate([jnp.where(first_head, xp, zero), jnp.where(first_head, zero, xp)], axis=0)
                y_diag.append(_dot(jnp.concatenate(ms, axis=1), rhs))
            y_off_b.append(_dot(cg, hb[:, g * GROUP_W:(g + 1) * GROUP_W]))
        y_diag = jnp.concatenate(y_diag, axis=1)
        y_off_b = jnp.concatenate(y_off_b, axis=1)

        diag_h = jnp.zeros((CHUNK, LANES), F32)
        for g in range(N_SSM_GROUPS):
            diag_h = jnp.where(grp == g, cb_diag[g], diag_h)
        coef = dskip_ref[...] + diag_h * dt
        scale = jnp.exp(acs)
        wgt = jnp.exp(tot - pre) * dt
        dec = jnp.exp(jnp.broadcast_to(tot, (BF16_ROWS, LANES)))
        pk_scale = _pack_hilo(scale, lane)
        scale_f = _dot(pk_scale, ef)
        scale_b = _dot(pk_scale, eb)
        coef_x = _dot(_pack_hilo(coef, lane), eb)
        pk_wd = _pack_hilo(jnp.concatenate([wgt, dec], axis=0),
                           lax.broadcasted_iota(jnp.int32, (CHUNK + BF16_ROWS, LANES), 1))
        wd_x = _dot(pk_wd, ef)
        xf = xb.astype(F32)
        y_part = y_diag + y_off_b * scale_b + coef_x * xf
        xw = (xf * wd_x[:CHUNK]).astype(BF16)
        upd = [_dot_tn(b_ref[0, rows, g * D_STATE:(g + 1) * D_STATE], xw[:, g * GROUP_W:(g + 1) * GROUP_W])
               for g in range(N_SSM_GROUPS)]
        return y_part, scale_f, wd_x[CHUNK:CHUNK + 1], upd

    cps = x_ref.shape[1] // CHUNK
    parts = [state_free_part(u) for u in range(cps)]
    for u in range(cps):
        rows = slice(u * CHUNK, (u + 1) * CHUNK)
        y_part, scale_f, decx, upd = parts[u]
        for g in range(N_SSM_GROUPS):
            cols = slice(g * GROUP_W, (g + 1) * GROUP_W)
            cg = c_ref[0, rows, g * D_STATE:(g + 1) * D_STATE]
            y_off_f = _dot(cg, h_scr[:, cols].astype(BF16))
            yg = (y_part[:, cols] + y_off_f * scale_f[:, cols]) * z_ref[0, rows, cols].astype(F32)
            ms = jnp.mean(yg * yg, axis=-1, keepdims=True)
            y_ref[0, rows, cols] = (yg * lax.rsqrt(ms + RMS_EPS) * nw_ref[:, cols]).astype(BF16)
            h_scr[:, cols] = h_scr[:, cols] * decx[:, cols] + upd[g]


def _ssd_fwd_call(xs, bm, cm, dt, zsil, hb, alog, dskip, nw, tril, ef, eb, sel):
    b, s, _ = xs.shape
    cps = min(SSD_FWD_CHUNKS_PER_STEP, s // CHUNK)
    assert s % (cps * CHUNK) == 0
    nc = s // (cps * CHUNK)
    fwd = lambda i, c: (i, c, 0)
    const = lambda i, c: (0, 0)
    return pl.pallas_call(
        _ssd_fwd_kernel,
        grid=(b, nc),
        in_specs=[
            pl.BlockSpec((1, cps * CHUNK, D_INNER), fwd),
            pl.BlockSpec((1, cps * CHUNK, BC_W), fwd),
            pl.BlockSpec((1, cps * CHUNK, BC_W), fwd),
            pl.BlockSpec((1, cps * CHUNK, LANES), fwd),
            pl.BlockSpec((1, cps * CHUNK, D_INNER), fwd),
            pl.BlockSpec((1, cps, D_STATE, D_INNER), lambda i, c: (i, c, 0, 0)),
            pl.BlockSpec(alog.shape, const),
            pl.BlockSpec(dskip.shape, const),
            pl.BlockSpec(nw.shape, const),
            pl.BlockSpec(tril.shape, const),
            pl.BlockSpec(ef.shape, const),
            pl.BlockSpec(eb.shape, const),
            pl.BlockSpec(sel.shape, const),
        ],
        out_specs=pl.BlockSpec((1, cps * CHUNK, D_INNER), fwd),
        out_shape=jax.ShapeDtypeStruct((b, s, D_INNER), BF16),
        scratch_shapes=[pltpu.VMEM((D_STATE, D_INNER), F32)],
        compiler_params=_cparams(("parallel", "arbitrary")),
        name="ssd_fwd",
    )(xs, bm, cm, dt, zsil, hb, alog, dskip, nw, tril, ef, eb, sel)


def _out_kernel(x_ref, att_ref, y_ref, gate_ref, wa_ref, ws_ref, wo_ref, lg_ref, lb_ref, o_ref):
    att = _dot(att_ref[0], wa_ref[...])
    ssm = _dot(y_ref[0], ws_ref[...])
    ga = gate_ref[0, :, :D_MODEL].astype(F32)
    gs = gate_ref[0, :, D_MODEL:].astype(F32)
    mixed = ga * att + gs * ssm
    out = _dot(mixed.astype(BF16), wo_ref[...])
    r = ALPHA * x_ref[0] + out
    mu = jnp.mean(r, axis=-1, keepdims=True)
    d = r - mu
    var = jnp.mean(d * d, axis=-1, keepdims=True)
    o_ref[0] = d * lax.rsqrt(var + LN_EPS) * lg_ref[...] + lb_ref[...]


def _out_call(x, att, y, gates, wa, ws, wo, lg, lb, tm):
    b, s, _ = x.shape
    tok = lambda i, j: (i, j, 0)
    const = lambda i, j: (0, 0)
    return pl.pallas_call(
        _out_kernel,
        grid=(b, s // tm),
        in_specs=[
            pl.BlockSpec((1, tm, D_MODEL), tok),
            pl.BlockSpec((1, tm, ATT_W), tok),
            pl.BlockSpec((1, tm, D_INNER), tok),
            pl.BlockSpec((1, tm, 2 * D_MODEL), tok),
            pl.BlockSpec(wa.shape, const),
            pl.BlockSpec(ws.shape, const),
            pl.BlockSpec(wo.shape, const),
            pl.BlockSpec(lg.shape, const),
            pl.BlockSpec(lb.shape, const),
        ],
        out_specs=pl.BlockSpec((1, tm, D_MODEL), tok),
        out_shape=jax.ShapeDtypeStruct((b, s, D_MODEL), F32),
        compiler_params=_cparams(("parallel", "parallel")),
        name="merge_out_ln",
    )(x, att, y, gates, wa, ws, wo, lg, lb)


def _rope_tables(seq_len):
    rows = seq_len // GRID_W
    row = jnp.repeat(jnp.arange(rows, dtype=F32), GRID_W)
    col = jnp.tile(jnp.arange(GRID_W, dtype=F32), rows)
    freqs = ROPE_THETA ** (-jnp.arange(0, AXIS_DIM, 2, dtype=F32) / AXIS_DIM)
    ang = jnp.concatenate([row[:, None] * freqs, col[:, None] * freqs], axis=-1)
    return jnp.cos(ang).T, jnp.sin(ang).T


def _expand_matrix(lane_offset):
    r = np.arange(LANES)[:, None] % (LANES // 2)
    head = np.arange(D_INNER)[None, :] // SSM_HEAD_DIM
    return jnp.asarray((r == head + lane_offset).astype(np.float32), dtype=BF16)


def _select_matrices():
    lane = np.arange(2 * LANES)[:, None] % LANES
    second = (np.arange(2 * CHUNK)[None, :] >= CHUNK).astype(np.int64)
    mats = [(lane == N_SSM_HEADS + 2 * p + second) for p in range(N_SSM_HEADS // 2)]
    return jnp.asarray(np.concatenate(mats, axis=1).astype(np.float32), dtype=BF16)


def _layer(x, p, tables):
    b, s, _ = x.shape
    tm = min(MAX_TOKEN_TILE, s)
    tq = min(MAX_QUERY_TILE, s)
    cos_t, sin_t = tables
    qt, k, vt = _qkv_call(x, p["wt_qkv"], cos_t[:, :s], sin_t[:, :s], p["qw"][:, :tm], p["kw"][:, :tm], tm)
    gsil, zsil, gates, dt = _gates_call(x, p["w_gates"], p["b_gate"], p["b_dt"], min(GATES_TOKEN_TILE, s))
    xs, bm, cm = _xbc_call(x, p["w_xbc"], p["conv_w"], p["conv_b"], min(XBC_TOKEN_TILE, s))
    att = _attn_call(qt, k, vt, gsil, tq)
    hb = _ssd_bwd_call(xs, bm, dt, p["alog"], p["tril"], p["eb"])
    y = _ssd_fwd_call(xs, bm, cm, dt, zsil, hb, p["alog"], p["dskip"], p["nw"], p["tril"], p["ef"], p["eb"],
                      p["sel"])
    return _out_call(x, att, y, gates, p["wa"], p["ws"], p["wo"], p["lg"], p["lb"], tm)


def _prepare(w_in, b_gate, q_norm_w, k_norm_w, conv_w, conv_b, dt_bias_fwd, dt_bias_bwd,
             a_log_fwd, a_log_bwd, d_skip, ssm_norm_w, w_att_proj, w_ssm_proj, w_out, ln_g, ln_b):
    wq, wk, wv, wg, wz, wxbc, wdt, wgate = jnp.split(w_in, SPLIT_POINTS, axis=-1)
    perm = np.concatenate([np.arange(0, HEAD_DIM, 2), np.arange(1, HEAD_DIM, 2)])
    qperm = (np.arange(N_Q_HEADS)[:, None] * HEAD_DIM + perm[None, :]).reshape(-1)
    kperm = (np.arange(N_KV_HEADS)[:, None] * HEAD_DIM + perm[None, :]).reshape(-1)
    pad = LANES - 2 * N_SSM_HEADS
    zeros_h = jnp.zeros((N_SSM_HEADS,), F32)
    zeros_p = jnp.zeros((pad,), F32)
    return {
        "wt_qkv": jnp.concatenate([wq[:, qperm], wk[:, kperm], wv], axis=1).T.astype(BF16),
        "qw": jnp.broadcast_to(q_norm_w[perm][:, None], (HEAD_DIM, MAX_TOKEN_TILE)),
        "kw": jnp.broadcast_to(k_norm_w[perm][:, None], (HEAD_DIM, MAX_TOKEN_TILE)),
        "w_gates": jnp.concatenate([wg, wz, wgate, wdt, jnp.zeros((D_MODEL, pad), F32)], axis=1).astype(BF16),
        "w_xbc": wxbc.astype(BF16),
        "conv_w": conv_w,
        "conv_b": conv_b[None, :],
        "b_gate": b_gate[None, :],
        "b_dt": jnp.concatenate([dt_bias_fwd, dt_bias_bwd, zeros_p])[None, :],
        "alog": jnp.concatenate([a_log_fwd, a_log_bwd, zeros_p])[None, :],
        "dskip": jnp.concatenate([zeros_h, d_skip, zeros_p])[None, :],
        "nw": ssm_norm_w[None, :],
        "tril": jnp.asarray(np.tril(np.ones((CHUNK, CHUNK), np.float32)), dtype=BF16),
        "ef": _expand_matrix(0),
        "eb": _expand_matrix(N_SSM_HEADS),
        "sel": _select_matrices(),
        "wa": w_att_proj.astype(BF16),
        "ws": w_ssm_proj.astype(BF16),
        "wo": w_out.astype(BF16),
        "lg": ln_g[None, :],
        "lb": ln_b[None, :],
    }


def kernel(x_prompt, x_sample, w_in, b_gate, q_norm_w, k_norm_w, conv_w, conv_b, dt_bias_fwd,
           dt_bias_bwd, a_log_fwd, a_log_bwd, d_skip, ssm_norm_w, w_att_proj, w_ssm_proj, w_out,
           ln_g, ln_b):
    xp, xs = x_prompt, x_sample
    tables = _rope_tables(max(xp.shape[1], xs.shape[1]))
    for l in range(w_in.shape[0]):
        p = _prepare(w_in[l], b_gate[l], q_norm_w[l], k_norm_w[l], conv_w[l], conv_b[l],
                     dt_bias_fwd[l], dt_bias_bwd[l], a_log_fwd[l], a_log_bwd[l], d_skip[l],
                     ssm_norm_w[l], w_att_proj[l], w_ssm_proj[l], w_out[l], ln_g[l], ln_b[l])
        xp = _layer(xp, p, tables)
        xs = _layer(xs, p, tables)
    return (xp, xs)
```
